```python
import jax, jax.numpy as jnp
from jax import lax
import numpy as np


D_MODEL = 1024
BATCH = 8
SEQ = 8192
DEPTH = 4

CTX_LEN = 256
GRID_W = 64
N_MIXERS = 3
N_SGU_LAYERS = (DEPTH + 2) // 3
N_RWKV_LAYERS = (DEPTH + 1) // 3
N_MLA_LAYERS = DEPTH // 3
RMS_EPS = 1e-6
SGU_WIDTH = 2 * D_MODEL
SGU_CHUNK = 128
SGU_GROUPS = 8
SGU_GROUP_W = SGU_WIDTH // SGU_GROUPS
RWKV_HEAD = 64
RWKV_WIDTH = D_MODEL
RWKV_HEADS = RWKV_WIDTH // RWKV_HEAD
RWKV_DECAY_LORA = 64
RWKV_ICLR_LORA = 64
RWKV_LN_EPS = 64e-5
RWKV_N_LERP = 6
MLA_HEADS = 16
MLA_NOPE = 128
MLA_ROPE = 64
MLA_QK_DIM = MLA_NOPE + MLA_ROPE
MLA_VDIM = 128
MLA_Q_RANK = 768
MLA_KV_RANK = 256
MLA_WIDTH = MLA_HEADS * MLA_VDIM
MLA_QBLOCK = 128
MLA_SCALE = MLA_QK_DIM ** -0.5
ROPE_BASE = 10000.0
ROPE_FREQS_PER_AXIS = MLA_ROPE // 4

kernel_name = 'hybrid_sgu_rwkv7_mla_dit_trunk'


def rms_norm(t, gain=None):
    tf = t.astype(jnp.float32)
    y = tf * lax.rsqrt(jnp.mean(tf * tf, axis=-1, keepdims=True) + RMS_EPS)
    if gain is not None:
        y = y * gain.astype(jnp.float32)
    return y.astype(t.dtype)


def ada_modulation(cond, w, b):
    m = jax.nn.silu(cond) @ w + b
    shift, scale, gate = jnp.split(m, 3, axis=-1)
    return shift[..., None, :], scale[..., None, :], gate[..., None, :]


def sgu_mixer(h, w_in, gain, w_s, b_s, w_out):
    bsz, length, _ = h.shape
    u, v, z = jnp.split(h @ w_in, 3, axis=-1)
    u = jax.nn.gelu(u)
    v = rms_norm(jax.nn.gelu(v), gain)
    vc = v.reshape(bsz, length // SGU_CHUNK, SGU_CHUNK, SGU_GROUPS, SGU_GROUP_W)
    mixed = jnp.einsum('gpq,bcqgd->bcpgd', w_s, vc) + b_s.T[:, :, None]
    s = u * mixed.reshape(bsz, length, SGU_WIDTH)
    return (s * jax.nn.silu(z)) @ w_out


def split_heads(t):
    return t.reshape(t.shape[:-1] + (RWKV_HEADS, RWKV_HEAD))


def center_shift(h):
    prev = jnp.pad(h[:, :-1], ((0, 0), (1, 0), (0, 0)))
    nxt = jnp.pad(h[:, 1:], ((0, 0), (0, 1), (0, 0)))
    return 0.5 * (prev + nxt) - h


def rwkv7_features(h, mu, w_in, w_lora1, w_lora2, w0, a_lora1, a_lora2, a0, k_k, k_a):
    xx = center_shift(h)
    xs = h[:, :, None, :] + xx[:, :, None, :] * mu
    rkvz = jnp.einsum('blcd,cde->blce', xs[:, :, :4], w_in)
    r, k, v, z = rkvz[:, :, 0], rkvz[:, :, 1], rkvz[:, :, 2], rkvz[:, :, 3]
    w_lora = jnp.einsum('nblr,nre->nble', jnp.tanh(jnp.einsum('bld,ndr->nblr', xs[:, :, 4], w_lora1)), w_lora2)
    log_w = -jax.nn.softplus(-(w0[:, None, None, :] + w_lora)) - 0.5
    decay = jnp.exp(-jnp.exp(log_w.astype(jnp.float32)))
    a = jax.nn.sigmoid(a0[:, None, None, :] + jnp.einsum('nblr,nre->nble', jnp.einsum('bld,ndr->nblr', xs[:, :, 5], a_lora1), a_lora2))
    kk = split_heads(k * k_k).astype(jnp.float32)
    kk = (kk * lax.rsqrt(jnp.sum(kk * kk, axis=-1, keepdims=True) + 1e-12)).reshape(k.shape)
    k_dir = k * (1 + (a - 1) * k_a)
    return r, decay, k_dir, v, kk, a, z


def to_scan_order(t_dir):
    return jnp.stack([t_dir[0], jnp.flip(t_dir[1], axis=1)])


def rwkv7_scan(state0, r, decay, k_dir, v, kk, a, emit):
    bsz, length = v.shape[:2]
    both = lambda t: jnp.broadcast_to(t, (2,) + t.shape)
    inputs = (decay, k_dir, both(v), both(kk), a) + ((both(r),) if emit else ())
    seqs = tuple(jnp.moveaxis(split_heads(to_scan_order(t).astype(jnp.float32)), 2, 0) for t in inputs)

    def step(S, inp):
        w_t, k_t, v_t, kk_t, a_t = inp[:5]
        s_kk = jnp.einsum('nbhvk,nbhk->nbhv', S, kk_t)
        S = S * w_t[..., None, :] - s_kk[..., :, None] * (kk_t * a_t)[..., None, :] + v_t[..., :, None] * k_t[..., None, :]
        y_t = jnp.einsum('nbhvk,nbhk->nbhv', S, inp[5]) if emit else None
        return S, y_t

    s_final, ys = lax.scan(step, state0, seqs)
    if not emit:
        return s_final, None
    ys = jnp.moveaxis(ys, 0, 2)
    y = ys[0] + jnp.flip(ys[1], axis=1)
    return s_final, y.reshape(bsz, length, RWKV_WIDTH)


def rwkv7_output(y, r, k_dir, v, z, r_k, ln_gain, ln_bias, w_out):
    yh = split_heads(y)
    mean = jnp.mean(yh, axis=-1, keepdims=True)
    var = jnp.mean(jnp.square(yh - mean), axis=-1, keepdims=True)
    yn = ((yh - mean) * lax.rsqrt(var + RWKV_LN_EPS)).reshape(y.shape) * ln_gain + ln_bias
    bonus = jnp.sum(split_heads(r)[None] * split_heads(k_dir) * r_k, axis=-1, keepdims=True) * split_heads(v)[None]
    out = (yn + jnp.sum(bonus, axis=0).reshape(y.shape)).astype(z.dtype)
    return (out * jax.nn.silu(z)) @ w_out


def axial_rope_tables(length):
    rows = length // GRID_W
    row = jnp.repeat(jnp.arange(rows, dtype=jnp.float32), GRID_W)
    col = jnp.tile(jnp.arange(GRID_W, dtype=jnp.float32), rows)
    inv_freq = ROPE_BASE ** (-jnp.arange(ROPE_FREQS_PER_AXIS, dtype=jnp.float32) / ROPE_FREQS_PER_AXIS)
    ang = jnp.concatenate([row[:, None] * inv_freq, col[:, None] * inv_freq], axis=-1)
    return jnp.cos(ang), jnp.sin(ang)


def apply_rope(t, cos, sin):
    t_nope, t_rope = jnp.split(t, [MLA_NOPE], axis=-1)
    x1, x2 = jnp.split(t_rope, 2, axis=-1)
    cs, sn = cos[:, None, :], sin[:, None, :]
    out = jnp.concatenate([t_nope, x1 * cs - x2 * sn, x1 * sn + x2 * cs], axis=-1)
    return out.astype(t.dtype)


def mla_project(h, w_in, q_norm, kv_norm, w_uq, w_ukv, qk_gain_q, qk_gain_k, need_q):
    bsz, length, _ = h.shape
    if need_q:
        c_q, c_kv, k_rope, z = jnp.split(h @ w_in, [MLA_Q_RANK, MLA_Q_RANK + MLA_KV_RANK, MLA_Q_RANK + MLA_KV_RANK + MLA_ROPE], axis=-1)
        q = (rms_norm(c_q, q_norm) @ w_uq).reshape(bsz, length, MLA_HEADS, MLA_QK_DIM)
        q = rms_norm(q, qk_gain_q)
    else:
        c_kv, k_rope = jnp.split(h @ w_in[:, MLA_Q_RANK:MLA_Q_RANK + MLA_KV_RANK + MLA_ROPE], [MLA_KV_RANK], axis=-1)
        q, z = None, None
    kv = (rms_norm(c_kv, kv_norm) @ w_ukv).reshape(bsz, length, MLA_HEADS, MLA_NOPE + MLA_VDIM)
    k_nope, v = jnp.split(kv, [MLA_NOPE], axis=-1)
    k = jnp.concatenate([k_nope, jnp.broadcast_to(k_rope[:, :, None, :], (bsz, length, MLA_HEADS, MLA_ROPE))], axis=-1)
    k = rms_norm(k, qk_gain_k)
    return q, k, v, z


def mla_latent_attention(q, k, v, k_ctx, v_ctx, cos, sin):
    bsz, length = q.shape[:2]
    n_ctx = k_ctx.shape[1]
    q_rot, k_rot = apply_rope(q, cos, sin), apply_rope(k, cos, sin)
    nblk = length // MLA_QBLOCK
    blocks = lambda t: jnp.moveaxis(t.reshape(bsz, nblk, MLA_QBLOCK, MLA_HEADS, MLA_QK_DIM), 1, 0)

    def one_block(qs):
        qr, qp = qs
        s = jnp.concatenate([jnp.einsum('bqhd,bkhd->bhqk', qp, k_ctx), jnp.einsum('bqhd,bkhd->bhqk', qr, k_rot)], axis=-1)
        p = jax.nn.softmax(s.astype(jnp.float32) * MLA_SCALE, axis=-1).astype(v.dtype)
        return jnp.einsum('bhqk,bkhd->bqhd', p[..., :n_ctx], v_ctx) + jnp.einsum('bhqk,bkhd->bqhd', p[..., n_ctx:], v)

    o = lax.map(one_block, (blocks(q_rot), blocks(q)))
    return jnp.moveaxis(o, 0, 1).reshape(bsz, length, MLA_WIDTH)


def mla_context_attention(q, k, v):
    bsz, n_ctx = q.shape[:2]
    p = jax.nn.softmax(jnp.einsum('bqhd,bkhd->bhqk', q, k).astype(jnp.float32) * MLA_SCALE, axis=-1).astype(v.dtype)
    return jnp.einsum('bhqk,bkhd->bqhd', p, v).reshape(bsz, n_ctx, MLA_WIDTH)


def setup_inputs(seed: int = 0) -> dict:
    key = jax.random.key(seed)
    it = iter(jax.random.split(key, 48))
    nrm = lambda shape, scale: scale * jax.random.normal(next(it), shape, jnp.float32)
    D = D_MODEL
    nA, nB, nC = N_SGU_LAYERS, N_RWKV_LAYERS, N_MLA_LAYERS
    return {
        'x': nrm((BATCH, SEQ, D), 1.0),
        'c': nrm((BATCH, D), 1.0),
        'ctx': nrm((BATCH, CTX_LEN, D), 1.0),
        'c_ctx': nrm((D,), 1.0),
        'ada_w': nrm((DEPTH, D, 3 * D), 0.5 * D ** -0.5),
        'ada_b': nrm((DEPTH, 3 * D), 0.02),
        'sgu_w_in': nrm((nA, D, 3 * SGU_WIDTH), D ** -0.5),
        'sgu_gain': 1.0 + nrm((nA, SGU_WIDTH), 0.02),
        'sgu_w_s': nrm((nA, SGU_GROUPS, SGU_CHUNK, SGU_CHUNK), SGU_CHUNK ** -0.5),
        'sgu_b_s': 1.0 + nrm((nA, SGU_GROUPS, SGU_CHUNK), 0.02),
        'sgu_w_out': nrm((nA, SGU_WIDTH, D), SGU_WIDTH ** -0.5),
        'rwkv_mu': jax.random.uniform(next(it), (nB, RWKV_N_LERP, D), jnp.float32),
        'rwkv_w_in': nrm((nB, 4, D, RWKV_WIDTH), D ** -0.5),
        'rwkv_w_lora1': nrm((nB, 2, D, RWKV_DECAY_LORA), D ** -0.5),
        'rwkv_w_lora2': nrm((nB, 2, RWKV_DECAY_LORA, RWKV_WIDTH), 0.5 * RWKV_DECAY_LORA ** -0.5),
        'rwkv_w0': -1.0 + nrm((nB, 2, RWKV_WIDTH), 0.5),
        'rwkv_a_lora1': nrm((nB, 2, D, RWKV_ICLR_LORA), D ** -0.5),
        'rwkv_a_lora2': nrm((nB, 2, RWKV_ICLR_LORA, RWKV_WIDTH), 0.5 * RWKV_ICLR_LORA ** -0.5),
        'rwkv_a0': nrm((nB, 2, RWKV_WIDTH), 0.1),
        'rwkv_k_k': 0.85 + nrm((nB, RWKV_WIDTH), 0.02),
        'rwkv_k_a': 1.0 + nrm((nB, RWKV_WIDTH), 0.02),
        'rwkv_r_k': nrm((nB, RWKV_HEADS, RWKV_HEAD), 0.1),
        'rwkv_ln_gain': 1.0 + nrm((nB, RWKV_WIDTH), 0.02),
        'rwkv_ln_bias': nrm((nB, RWKV_WIDTH), 0.02),
        'rwkv_w_out': nrm((nB, RWKV_WIDTH, D), RWKV_WIDTH ** -0.5),
        'mla_w_in': nrm((nC, D, MLA_Q_RANK + MLA_KV_RANK + MLA_ROPE + MLA_WIDTH), D ** -0.5),
        'mla_q_norm': 1.0 + nrm((nC, MLA_Q_RANK), 0.02),
        'mla_kv_norm': 1.0 + nrm((nC, MLA_KV_RANK), 0.02),
        'mla_w_uq': nrm((nC, MLA_Q_RANK, MLA_HEADS * MLA_QK_DIM), MLA_Q_RANK ** -0.5),
        'mla_w_ukv': nrm((nC, MLA_KV_RANK, MLA_HEADS * (MLA_NOPE + MLA_VDIM)), MLA_KV_RANK ** -0.5),
        'mla_qk_gain_q': 1.0 + nrm((nC, MLA_QK_DIM), 0.02),
        'mla_qk_gain_k': 1.0 + nrm((nC, MLA_QK_DIM), 0.02),
        'mla_w_out': nrm((nC, MLA_WIDTH, D), MLA_WIDTH ** -0.5),
    }


def reference(x, c, ctx, c_ctx, ada_w, ada_b, sgu_w_in, sgu_gain, sgu_w_s, sgu_b_s, sgu_w_out,
              rwkv_mu, rwkv_w_in, rwkv_w_lora1, rwkv_w_lora2, rwkv_w0, rwkv_a_lora1, rwkv_a_lora2, rwkv_a0,
              rwkv_k_k, rwkv_k_a, rwkv_r_k, rwkv_ln_gain, rwkv_ln_bias, rwkv_w_out,
              mla_w_in, mla_q_norm, mla_kv_norm, mla_w_uq, mla_w_ukv, mla_qk_gain_q, mla_qk_gain_k, mla_w_out):
    cos, sin = axial_rope_tables(x.shape[1])
    ctx_readers = [i for i in range(DEPTH) if i % N_MIXERS != 0]
    last_ctx_reader = ctx_readers[-1] if ctx_readers else -1
    for i in range(DEPTH):
        kind, j = i % N_MIXERS, i // N_MIXERS
        update_ctx = i < last_ctx_reader
        shift, scale, gate = ada_modulation(c, ada_w[i], ada_b[i])
        h = rms_norm(x) * (1 + scale) + shift
        if kind != 0 or update_ctx:
            c_shift, c_scale, c_gate = ada_modulation(c_ctx, ada_w[i], ada_b[i])
            hc = rms_norm(ctx) * (1 + c_scale) + c_shift
        if kind == 0:
            sgu_args = (sgu_w_in[j], sgu_gain[j], sgu_w_s[j], sgu_b_s[j], sgu_w_out[j])
            x = x + gate * sgu_mixer(h, *sgu_args)
            if update_ctx:
                ctx = ctx + c_gate * sgu_mixer(hc, *sgu_args)
        elif kind == 1:
            feat_args = (rwkv_mu[j], rwkv_w_in[j], rwkv_w_lora1[j], rwkv_w_lora2[j], rwkv_w0[j],
                         rwkv_a_lora1[j], rwkv_a_lora2[j], rwkv_a0[j], rwkv_k_k[j], rwkv_k_a[j])
            out_args = (rwkv_r_k[j], rwkv_ln_gain[j], rwkv_ln_bias[j], rwkv_w_out[j])
            r_c, w_c, kd_c, v_c, kk_c, a_c, z_c = rwkv7_features(hc, *feat_args)
            state0 = jnp.zeros((2, hc.shape[0], RWKV_HEADS, RWKV_HEAD, RWKV_HEAD), jnp.float32)
            s_ctx, y_c = rwkv7_scan(state0, r_c, w_c, kd_c, v_c, kk_c, a_c, emit=update_ctx)
            r_l, w_l, kd_l, v_l, kk_l, a_l, z_l = rwkv7_features(h, *feat_args)
            _, y_l = rwkv7_scan(s_ctx, r_l, w_l, kd_l, v_l, kk_l, a_l, emit=True)
            x = x + gate * rwkv7_output(y_l, r_l, kd_l, v_l, z_l, *out_args)
            if update_ctx:
                ctx = ctx + c_gate * rwkv7_output(y_c, r_c, kd_c, v_c, z_c, *out_args)
        else:
            mla_args = (mla_w_in[j], mla_q_norm[j], mla_kv_norm[j], mla_w_uq[j], mla_w_ukv[j],
                        mla_qk_gain_q[j], mla_qk_gain_k[j])
            q_c, k_c, v_c, z_c = mla_project(hc, *mla_args, need_q=update_ctx)
            q_l, k_l, v_l, z_l = mla_project(h, *mla_args, need_q=True)
            o_l = mla_latent_attention(q_l, k_l, v_l, k_c, v_c, cos, sin)
            x = x + gate * ((o_l * jax.nn.silu(z_l)) @ mla_w_out[j])
            if update_ctx:
                o_c = mla_context_attention(q_c, k_c, v_c)
                ctx = ctx + c_gate * ((o_c * jax.nn.silu(z_c)) @ mla_w_out[j])
    return x
```

```python
import functools
import math

import jax
import jax.numpy as jnp
from jax import lax
from jax.experimental import pallas as pl
from jax.experimental.pallas import tpu as pltpu

F32 = jnp.float32
BF16 = jnp.bfloat16
HIGHEST = lax.Precision.HIGHEST

N_MIXERS = 3
RMS_EPS = 1e-6
SGU_CHUNK = 128
SGU_GROUPS = 8
RWKV_HEAD = 64
RWKV_LN_EPS = 64e-5
RWKV_CHUNK = 64
MLA_HEADS = 16
MLA_NOPE = 128
MLA_ROPE = 64
MLA_VDIM = 128
MLA_Q_RANK = 768
MLA_KV_RANK = 256
MLA_QK_DIM = MLA_NOPE + MLA_ROPE
MLA_SCALE = MLA_QK_DIM ** -0.5
GRID_W = 64
ROPE_BASE = 10000.0

LANES = 128
SUBLANES = 8
VMEM_LIMIT = 56 * 1024 * 1024


def _params(n_axes):
    return pltpu.CompilerParams(dimension_semantics=("arbitrary",) * n_axes, vmem_limit_bytes=VMEM_LIMIT)


def _const_spec(shape):
    nd = len(shape)
    return pl.BlockSpec(shape, lambda *_: (0,) * nd, pipeline_mode=pl.Buffered(1))


def _dot(a, b, precision=None):
    return jnp.dot(a, b, preferred_element_type=F32, precision=precision)


def _dot_nt(a, b, precision=None):
    return lax.dot_general(a, b, (((1,), (1,)), ((), ())), preferred_element_type=F32, precision=precision)


def _dot_tn(a, b, precision=None):
    return lax.dot_general(a, b, (((0,), (0,)), ((), ())), preferred_element_type=F32, precision=precision)


def _sigmoid(x):
    return 1.0 / (1.0 + jnp.exp(-x))


def _silu(x):
    return x * _sigmoid(x)


def _gelu(x):
    return x * (0.5 * (1.0 + jnp.tanh(math.sqrt(2.0 / math.pi) * (x + 0.044715 * (x * x * x)))))


def _softplus(x):
    return jnp.maximum(x, 0.0) + jnp.log(1.0 + jnp.exp(-jnp.abs(x)))


def _rms(x, width=None):
    width = x.shape[-1] if width is None else width
    return x * lax.rsqrt(jnp.sum(x * x, axis=-1, keepdims=True) / width + RMS_EPS)


def _split_dot(x, w_bf16):
    hi = x.astype(BF16)
    lo = (x - hi.astype(F32)).astype(BF16)
    return _dot(hi, w_bf16) + _dot(lo, w_bf16)


def _head_sum(x, e_ref, et_ref):
    return _split_dot(_split_dot(x, e_ref[...]), et_ref[...])


def _head_indicator(width, head):
    e = (jnp.arange(width)[:, None] // head == jnp.arange(LANES)[None, :]).astype(BF16)
    return e, e.T


def _tile(length, pref):
    t = min(pref, length)
    assert length % t == 0, (length, t)
    return t


def _mod_spec(d):
    return pl.BlockSpec((None, 1, d), lambda b, t: (b, 0, 0))


def _ada_kernel(cond_ref, w_ref, b_ref, o_ref):
    c = cond_ref[...]
    o_ref[...] = _dot(_silu(c), w_ref[...], HIGHEST) + b_ref[...]


def _ada_modulation(cond, ada_w, ada_b):
    depth, d, d3 = ada_w.shape
    r = cond.shape[0]
    return pl.pallas_call(
        _ada_kernel,
        out_shape=jax.ShapeDtypeStruct((depth, r, d3), F32),
        grid=(depth, d3 // d),
        in_specs=[pl.BlockSpec((r, d), lambda i, j: (0, 0)),
                  pl.BlockSpec((None, d, d), lambda i, j: (i, 0, j)),
                  pl.BlockSpec((None, 1, d), lambda i, j: (i, 0, j))],
        out_specs=pl.BlockSpec((None, r, d), lambda i, j: (i, 0, j)),
        compiler_params=_params(2),
        name="ada_modulation",
    )(cond, ada_w, ada_b.reshape(depth, 1, d3))


def _sgu_kernel(x_ref, shift_ref, scale_ref, gate_ref, w_in_ref, gain_ref, w_s_ref, bias_ref, w_out_ref,
                o_ref, v_scr, s_scr):
    t = x_ref.shape[0]
    w = gain_ref.shape[-1]
    gw = w // SGU_GROUPS
    x = x_ref[...]
    hb = (_rms(x) * (1.0 + scale_ref[...]) + shift_ref[...]).astype(BF16)
    ssq = jnp.zeros((t, 1), F32)
    for g in range(SGU_GROUPS):
        vg = _gelu(_dot(hb, w_in_ref[:, w + g * gw:w + (g + 1) * gw]))
        v_scr[:, g * gw:(g + 1) * gw] = vg
        ssq = ssq + jnp.sum(vg * vg, axis=-1, keepdims=True)
    rs = lax.rsqrt(ssq / w + RMS_EPS)
    for g in range(SGU_GROUPS):
        cols = slice(g * gw, (g + 1) * gw)
        vn = (v_scr[:, cols] * rs * gain_ref[:, cols]).astype(BF16)
        u = _gelu(_dot(hb, w_in_ref[:, cols]))
        sz = _silu(_dot(hb, w_in_ref[:, 2 * w + g * gw:2 * w + (g + 1) * gw]))
        for c in range(t // SGU_CHUNK):
            rows = slice(c * SGU_CHUNK, (c + 1) * SGU_CHUNK)
            mixed = _dot(w_s_ref[g], vn[rows]) + bias_ref[:, cols]
            s_scr[rows, cols] = ((u[rows] * mixed) * sz[rows]).astype(BF16)
    o_ref[...] = x + gate_ref[...] * _dot(s_scr[...], w_out_ref[...])


def _sgu_layer(x, shift, scale, gate, w_in, gain, w_s, b_s, w_out):
    b, l, d = x.shape
    w = gain.shape[-1]
    t = _tile(l, 512)
    assert t % SGU_CHUNK == 0 and w % (SGU_GROUPS * LANES) == 0
    bias = jnp.repeat(b_s.T, w // SGU_GROUPS, axis=1)
    tok = pl.BlockSpec((None, t, d), lambda bi, ti: (bi, ti, 0))
    return pl.pallas_call(
        _sgu_kernel,
        out_shape=jax.ShapeDtypeStruct(x.shape, F32),
        grid=(b, l // t),
        in_specs=[tok, _mod_spec(d), _mod_spec(d), _mod_spec(d),
                  _const_spec((d, 3 * w)), _const_spec((1, w)), _const_spec(w_s.shape),
                  _const_spec(bias.shape), _const_spec((w, d))],
        out_specs=tok,
        scratch_shapes=[pltpu.VMEM((t, w), F32), pltpu.VMEM((t, w), BF16)],
        compiler_params=_params(2),
        name="sgu_layer",
    )(x, shift, scale, gate, w_in.astype(BF16), gain.reshape(1, w), w_s.astype(BF16), bias, w_out.astype(BF16))


def _rwkv_feat_kernel(x_ref, xp_ref, xn_ref, shift_ref, scale_ref, mu_ref, w_in_ref, wl1_ref, wl2_ref, w0_ref,
                      al1_ref, al2_ref, a0_ref, kk_w_ref, ka_ref, rk_ref, e_ref, et_ref,
                      r_o, v_o, kk_o, z_o, bonus_o, lw_o, kd_o, bb_o):
    t = x_ref.shape[0]
    ti = pl.program_id(1)
    nt = pl.num_programs(1)
    sc = 1.0 + scale_ref[...]
    sh = shift_ref[...]
    h = _rms(x_ref[...]) * sc + sh
    hp = (_rms(xp_ref[SUBLANES - 1:SUBLANES, :]) * sc + sh) * (ti > 0).astype(F32)
    hn = (_rms(xn_ref[0:1, :]) * sc + sh) * (ti < nt - 1).astype(F32)
    row = lax.broadcasted_iota(jnp.int32, (t, 1), 0)
    prev = jnp.where(row == 0, hp, pltpu.roll(h, 1, 0))
    nxt = jnp.where(row == t - 1, hn, pltpu.roll(h, t - 1, 0))
    xx = 0.5 * (prev + nxt) - h

    def lerp(c):
        return (h + xx * mu_ref[c:c + 1, :]).astype(BF16)

    r = _dot(lerp(0), w_in_ref[0])
    k = _dot(lerp(1), w_in_ref[1])
    v = _dot(lerp(2), w_in_ref[2])
    z_o[...] = _dot(lerp(3), w_in_ref[3])
    r_o[...] = r
    v_o[...] = v
    kkr = k * kk_w_ref[...]
    kkn = kkr * _split_dot(lax.rsqrt(_split_dot(kkr * kkr, e_ref[...]) + 1e-12), et_ref[...])
    kk_o[...] = kkn
    t_w = jnp.tanh(_dot(lerp(4), wl1_ref[...])).astype(BF16)
    t_a = _dot(lerp(5), al1_ref[...]).astype(BF16)
    kd_sum = jnp.zeros_like(k)
    for n in range(2):
        log_w = -_softplus(-(w0_ref[n:n + 1, :] + _dot(t_w, wl2_ref[n]))) - 0.5
        lw_o[n] = -jnp.exp(log_w)
        a = _sigmoid(a0_ref[n:n + 1, :] + _dot(t_a, al2_ref[n]))
        kd = k * (1.0 + (a - 1.0) * ka_ref[...])
        kd_o[n] = kd
        bb_o[n] = kkn * a
        kd_sum = kd_sum + kd
    bonus_o[...] = _head_sum(r * kd_sum * rk_ref[...], e_ref, et_ref) * v


def _rwkv_features(x, shift, scale, mu, w_in, w_lora1, w_lora2, w0, a_lora1, a_lora2, a0, k_k, k_a, r_k):
    b, l, d = x.shape
    e = w_in.shape[-1]
    t = _tile(l, 256)
    nb8 = l // SUBLANES
    tok = pl.BlockSpec((None, t, d), lambda bi, ti: (bi, ti, 0))
    prev8 = pl.BlockSpec((None, SUBLANES, d), lambda bi, ti: (bi, jnp.maximum(ti * (t // SUBLANES) - 1, 0), 0))
    next8 = pl.BlockSpec((None, SUBLANES, d), lambda bi, ti: (bi, jnp.minimum((ti + 1) * (t // SUBLANES), nb8 - 1), 0))
    rank = w_lora1.shape[-1]
    assert 2 * rank == LANES and a_lora1.shape[-1] == rank
    zr = jnp.zeros((rank, e), F32)
    wl1 = jnp.concatenate([w_lora1[0], w_lora1[1]], axis=1).astype(BF16)
    al1 = jnp.concatenate([a_lora1[0], a_lora1[1]], axis=1).astype(BF16)
    wl2 = jnp.stack([jnp.concatenate([w_lora2[0], zr]), jnp.concatenate([zr, w_lora2[1]])]).astype(BF16)
    al2 = jnp.stack([jnp.concatenate([a_lora2[0], zr]), jnp.concatenate([zr, a_lora2[1]])]).astype(BF16)
    ind, ind_t = _head_indicator(e, RWKV_HEAD)
    out_tok = pl.BlockSpec((None, t, e), lambda bi, ti: (bi, ti, 0))
    out_dir = pl.BlockSpec((2, None, t, e), lambda bi, ti: (0, bi, ti, 0))
    shared = jax.ShapeDtypeStruct((b, l, e), F32)
    per_dir = jax.ShapeDtypeStruct((2, b, l, e), F32)
    return pl.pallas_call(
        _rwkv_feat_kernel,
        out_shape=(shared,) * 5 + (per_dir,) * 3,
        grid=(b, l // t),
        in_specs=[tok, prev8, next8, _mod_spec(d), _mod_spec(d),
                  _const_spec(mu.shape), _const_spec(w_in.shape), _const_spec(wl1.shape), _const_spec(wl2.shape),
                  _const_spec(w0.shape), _const_spec(al1.shape), _const_spec(al2.shape), _const_spec(a0.shape),
                  _const_spec((1, e)), _const_spec((1, e)), _const_spec((1, e)),
                  _const_spec(ind.shape), _const_spec(ind_t.shape)],
        out_specs=(out_tok,) * 5 + (out_dir,) * 3,
        compiler_params=_params(2),
        name="rwkv_features",
    )(x, x, x, shift, scale, mu, w_in.astype(BF16), wl1, wl2, w0, al1, al2, a0,
      k_k.reshape(1, e), k_a.reshape(1, e), r_k.reshape(1, e), ind, ind_t)


def _unit_triangular_inverse(a, eye, r2, c2, size):
    d = jnp.where((r2 >> 3) == (c2 >> 3), a, 0.0)
    d2 = _dot(d, d, HIGHEST)
    d4 = _dot(d2, d2, HIGHEST)
    inv = _dot(_dot(eye - d, eye + d2, HIGHEST), eye + d4, HIGHEST)
    s = 8
    while s < size:
        sh = s.bit_length() - 1
        off = ((r2 >> (sh + 1)) == (c2 >> (sh + 1))) & ((r2 >> sh) != (c2 >> sh))
        inv = inv - _dot(inv, _dot(jnp.where(off, a, 0.0), inv, HIGHEST), HIGHEST)
        s *= 2
    return inv


def _rwkv_scan_kernel(lw_ref, kd_ref, bb_ref, v_ref, kk_ref, r_ref, s0_ref, y_ref, sf_ref, st_scr):
    c, hn = v_ref.shape
    half = LANES // 2
    assert c == half == RWKV_HEAD
    ci = pl.program_id(2)
    nc = pl.num_programs(2)

    @pl.when(ci == 0)
    def _():
        st_scr[...] = s0_ref[...]

    sgn = jnp.where(pl.program_id(0) == 0, 1, -1)
    rc = lax.broadcasted_iota(jnp.int32, (c, c), 0)
    cc = lax.broadcasted_iota(jnp.int32, (c, c), 1)
    cum = (sgn * (rc - cc) >= 0).astype(F32)
    lw = lw_ref[...]
    cin = _dot(cum, lw, HIGHEST)
    ctot = _dot_tn(lw, jnp.ones((c, LANES), F32), HIGHEST)
    p_in = jnp.exp(cin)
    inv_in = jnp.exp(-cin)
    p_rest = jnp.exp(jnp.sum(lw, axis=0, keepdims=True) - cin)
    rt = r_ref[...] * p_in
    kt = kk_ref[...] * jnp.exp(cin - lw)
    kbar = kd_ref[...] * inv_in
    bbar = bb_ref[...] * inv_in
    khat = kd_ref[...] * p_rest
    bhat = bb_ref[...] * p_rest
    v = v_ref[...]

    n2 = 2 * c
    r2 = lax.broadcasted_iota(jnp.int32, (n2, n2), 0)
    c2 = lax.broadcasted_iota(jnp.int32, (n2, n2), 1)
    tdiff = sgn * ((r2 & (c - 1)) - (c2 & (c - 1)))
    strict = tdiff > 0
    incl = tdiff >= 0
    eye = (r2 == c2).astype(F32)
    first = lax.broadcasted_iota(jnp.int32, (1, LANES), 1) < half

    def stack(xp):
        return jnp.concatenate([jnp.where(first, xp, 0.0), jnp.where(first, 0.0, xp)], axis=0)

    for p in range(hn // LANES):
        sl = slice(p * LANES, (p + 1) * LANES)
        kt_s, rt_s, v_s = stack(kt[:, sl]), stack(rt[:, sl]), stack(v[:, sl])
        lhs = jnp.concatenate([kt_s, rt_s], axis=0)
        gb = _dot_nt(lhs, stack(bbar[:, sl]), HIGHEST)
        gk = _dot_nt(lhs, stack(kbar[:, sl]), HIGHEST)
        a_m = jnp.where(strict, gb[:n2], 0.0)
        qb = jnp.where(incl, gb[n2:], 0.0)
        b_m = jnp.where(strict, gk[:n2], 0.0)
        qk = jnp.where(incl, gk[n2:], 0.0)
        t_inv = _unit_triangular_inverse(a_m, eye, r2, c2, c)
        w12 = _dot(t_inv, jnp.concatenate([kt_s, _dot(b_m, v_s, HIGHEST)], axis=1), HIGHEST)
        state = st_scr[p]
        m1 = _dot(jnp.concatenate([rt_s, w12[:, :LANES]], axis=0), state, HIGHEST)
        u = m1[n2:] + w12[:, LANES:]
        vu = jnp.concatenate([v_s, u], axis=0)
        y = m1[:n2] + _dot(jnp.concatenate([qk, -qb], axis=1), vu, HIGHEST)
        y_ref[:, sl] = y[:c] + y[c:]
        upd = _dot_tn(jnp.concatenate([stack(khat[:, sl]), -stack(bhat[:, sl])], axis=0), vu, HIGHEST)
        st_scr[p] = jnp.exp(ctot[sl]) * state + upd

    @pl.when(ci == nc - 1)
    def _():
        sf_ref[...] = st_scr[...]


def _rwkv_scan(lw, kd, bb, v, kk, r, s0):
    _, b, l, hn = lw.shape
    c = RWKV_CHUNK
    nc = l // c
    npair = hn // LANES
    assert l % c == 0 and hn % LANES == 0

    def chunk(d, ci):
        return ci + d * (nc - 1 - 2 * ci)

    per_dir = pl.BlockSpec((None, None, c, hn), lambda d, bi, ci: (d, bi, chunk(d, ci), 0))
    shared = pl.BlockSpec((None, c, hn), lambda d, bi, ci: (bi, chunk(d, ci), 0))
    st = pl.BlockSpec((None, None, npair, LANES, LANES), lambda d, bi, ci: (d, bi, 0, 0, 0))
    return pl.pallas_call(
        _rwkv_scan_kernel,
        out_shape=(jax.ShapeDtypeStruct((2, b, l, hn), F32), jax.ShapeDtypeStruct(s0.shape, F32)),
        grid=(2, b, nc),
        in_specs=[per_dir, per_dir, per_dir, shared, shared, shared, st],
        out_specs=(per_dir, st),
        scratch_shapes=[pltpu.VMEM((npair, LANES, LANES), F32)],
        compiler_params=_params(3),
        name="rwkv_scan",
    )(lw, kd, bb, v, kk, r, s0)


def _rwkv_out_kernel(x_ref, y_ref, bonus_ref, z_ref, gate_ref, lng_ref, lnb_ref, e_ref, et_ref, w_out_ref, o_ref):
    y = y_ref[0] + y_ref[1]
    mean = _head_sum(y, e_ref, et_ref) * (1.0 / RWKV_HEAD)
    yc = y - mean
    var = _head_sum(yc * yc, e_ref, et_ref) * (1.0 / RWKV_HEAD)
    yn = yc * lax.rsqrt(var + RWKV_LN_EPS) * lng_ref[...] + lnb_ref[...]
    out = (yn + bonus_ref[...]) * _silu(z_ref[...])
    o_ref[...] = x_ref[...] + gate_ref[...] * _dot(out.astype(BF16), w_out_ref[...])


def _rwkv_output(x, y, bonus, z, gate, ln_gain, ln_bias, w_out):
    b, l, d = x.shape
    e = y.shape[-1]
    t = _tile(l, 512)
    ind, ind_t = _head_indicator(e, RWKV_HEAD)
    tok_d = pl.BlockSpec((None, t, d), lambda bi, ti: (bi, ti, 0))
    tok_e = pl.BlockSpec((None, t, e), lambda bi, ti: (bi, ti, 0))
    return pl.pallas_call(
        _rwkv_out_kernel,
        out_shape=jax.ShapeDtypeStruct(x.shape, F32),
        grid=(b, l // t),
        in_specs=[tok_d, pl.BlockSpec((2, None, t, e), lambda bi, ti: (0, bi, ti, 0)), tok_e, tok_e, _mod_spec(d),
                  _const_spec((1, e)), _const_spec((1, e)), _const_spec(ind.shape), _const_spec(ind_t.shape),
                  _const_spec(w_out.shape)],
        out_specs=tok_d,
        compiler_params=_params(2),
        name="rwkv_output",
    )(x, y, bonus, z, gate, ln_gain.reshape(1, e), ln_bias.reshape(1, e), ind, ind_t, w_out.astype(BF16))


def _mla_proj_kernel(*refs, need_q, rotate):
    if need_q:
        (x_ref, shift_ref, scale_ref, w_in_ref, qn_ref, kvn_ref, w_uq_ref, w_ukv_ref, gq_ref, gk_ref,
         ta_ref, tb_ref, q_o, k_o, v_o, z_o) = refs
    else:
        (x_ref, shift_ref, scale_ref, w_in_ref, kvn_ref, w_ukv_ref, gk_ref, ta_ref, tb_ref, k_o, v_o) = refs
    hb = (_rms(x_ref[...]) * (1.0 + scale_ref[...]) + shift_ref[...]).astype(BF16)
    o_kv = MLA_Q_RANK if need_q else 0
    o_kr = o_kv + MLA_KV_RANK
    first = lax.broadcasted_iota(jnp.int32, (1, LANES), 1) < MLA_ROPE
    ta = ta_ref[...]
    tb = tb_ref[...]

    def rope_tile(t2, gain_ref, use_rot, use_plain):
        sw = pltpu.roll(t2, MLA_ROPE, 1)
        g_a = gain_ref[1:2, :]
        g_b = gain_ref[2:3, :]
        m_b = tb * g_b
        if not use_rot:
            return sw * jnp.where(first, 0.0, m_b)
        if not use_plain:
            m_b = jnp.where(first, m_b, 0.0)
        return t2 * (ta * g_a) + sw * m_b

    ckv = _rms(_dot(hb, w_in_ref[:, o_kv:o_kr])) * kvn_ref[...]
    kv = _dot(ckv.astype(BF16), w_ukv_ref[...])
    kr = _dot(hb, w_in_ref[:, o_kr:o_kr + LANES])
    kr_ssq = jnp.sum(jnp.where(first, kr * kr, 0.0), axis=-1, keepdims=True)
    kr_out = rope_tile(kr, gk_ref, rotate, not rotate)
    for h in range(MLA_HEADS):
        kn = kv[:, h * 2 * LANES:h * 2 * LANES + LANES]
        rs = lax.rsqrt((jnp.sum(kn * kn, axis=-1, keepdims=True) + kr_ssq) / MLA_QK_DIM + RMS_EPS)
        k_o[h] = jnp.concatenate([kn * rs * gk_ref[0:1, :], kr_out * rs], axis=1).astype(k_o.dtype)
        v_o[h] = kv[:, h * 2 * LANES + LANES:(h + 1) * 2 * LANES].astype(v_o.dtype)
    if need_q:
        z_o[...] = _dot(hb, w_in_ref[:, o_kr + LANES:])
        cq = _rms(_dot(hb, w_in_ref[:, :MLA_Q_RANK])) * qn_ref[...]
        q = _dot(cq.astype(BF16), w_uq_ref[...])
        for h in range(MLA_HEADS):
            t1 = q[:, h * 2 * LANES:h * 2 * LANES + LANES]
            t2 = q[:, h * 2 * LANES + LANES:(h + 1) * 2 * LANES]
            ssq = jnp.sum(t1 * t1, axis=-1, keepdims=True) + jnp.sum(jnp.where(first, t2 * t2, 0.0), axis=-1, keepdims=True)
            rs = lax.rsqrt(ssq / MLA_QK_DIM + RMS_EPS) * MLA_SCALE
            q_o[h] = jnp.concatenate([t1 * rs * gq_ref[0:1, :], rope_tile(t2, gq_ref, True, True) * rs],
                                     axis=1).astype(q_o.dtype)


def _swap_halves(w):
    half = w.shape[-1] // 2
    return jnp.concatenate([w[..., half:], w[..., :half]], axis=-1)


def _qk_gain_rows(g):
    gn, gr = g[:MLA_NOPE], g[MLA_NOPE:]
    zero = jnp.zeros_like(gr)
    rows = jnp.stack([gn, jnp.concatenate([gr, zero]), jnp.concatenate([_swap_halves(gr), gr])])
    return jnp.concatenate([rows, jnp.zeros((SUBLANES - 3, LANES), F32)])


def _rope_tables(length):
    rows = length // GRID_W
    row = jnp.repeat(jnp.arange(rows, dtype=F32), GRID_W)
    col = jnp.tile(jnp.arange(GRID_W, dtype=F32), rows)
    nf = MLA_ROPE // 4
    inv_freq = ROPE_BASE ** (-jnp.arange(nf, dtype=F32) / nf)
    ang = jnp.concatenate([row[:, None] * inv_freq, col[:, None] * inv_freq], axis=-1)
    cos, sin = jnp.cos(ang), jnp.sin(ang)
    ta = jnp.concatenate([cos, cos, jnp.zeros((length, MLA_ROPE), F32)], axis=-1)
    tb = jnp.concatenate([-sin, sin, jnp.ones((length, MLA_ROPE), F32)], axis=-1)
    return ta, tb


def _mla_project(x, shift, scale, w_in, q_norm, kv_norm, w_uq, w_ukv, gq, gk, ta, tb, need_q, rotate):
    b, l, d = x.shape
    hh = MLA_HEADS
    t = _tile(l, 256)
    o1, o2, o3 = MLA_Q_RANK, MLA_Q_RANK + MLA_KV_RANK, MLA_Q_RANK + MLA_KV_RANK + MLA_ROPE
    w_kr = w_in[:, o2:o3]
    cols = ([w_in[:, :o1]] if need_q else []) + [w_in[:, o1:o2], w_kr, _swap_halves(w_kr)] + ([w_in[:, o3:]] if need_q else [])
    w_in_p = jnp.concatenate(cols, axis=1).astype(BF16)
    tok = pl.BlockSpec((None, t, d), lambda bi, ti: (bi, ti, 0))
    tab = pl.BlockSpec((t, LANES), lambda bi, ti: (ti, 0))
    head_qk = pl.BlockSpec((None, hh, t, 2 * LANES), lambda bi, ti: (bi, 0, ti, 0))
    head_v = pl.BlockSpec((None, hh, t, MLA_VDIM), lambda bi, ti: (bi, 0, ti, 0))
    qk_shape = jax.ShapeDtypeStruct((b, hh, l, 2 * LANES), BF16)
    v_shape = jax.ShapeDtypeStruct((b, hh, l, MLA_VDIM), BF16)
    kvn = kv_norm.reshape(1, -1)
    w_ukv_b = w_ukv.astype(BF16)
    gk_rows = _qk_gain_rows(gk)
    if need_q:
        wq = w_uq.reshape(MLA_Q_RANK, hh, MLA_QK_DIM)
        w_uq_p = jnp.concatenate([wq, _swap_halves(wq[..., MLA_NOPE:])], axis=-1).reshape(MLA_Q_RANK, hh * 2 * LANES)
        width = w_in.shape[1] - o3
        ins = (x, shift, scale, w_in_p, q_norm.reshape(1, -1), kvn, w_uq_p.astype(BF16), w_ukv_b,
               _qk_gain_rows(gq), gk_rows, ta, tb)
        in_specs = [tok, _mod_spec(d), _mod_spec(d), _const_spec(w_in_p.shape), _const_spec((1, MLA_Q_RANK)),
                    _const_spec(kvn.shape), _const_spec(w_uq_p.shape), _const_spec(w_ukv_b.shape),
                    _const_spec((SUBLANES, LANES)), _const_spec((SUBLANES, LANES)), tab, tab]
        out_shape = (qk_shape, qk_shape, v_shape, jax.ShapeDtypeStruct((b, l, width), F32))
        out_specs = (head_qk, head_qk, head_v, pl.BlockSpec((None, t, width), lambda bi, ti: (bi, ti, 0)))
    else:
        ins = (x, shift, scale, w_in_p, kvn, w_ukv_b, gk_rows, ta, tb)
        in_specs = [tok, _mod_spec(d), _mod_spec(d), _const_spec(w_in_p.shape), _const_spec(kvn.shape),
                    _const_spec(w_ukv_b.shape), _const_spec((SUBLANES, LANES)), tab, tab]
        out_shape = (qk_shape, v_shape)
        out_specs = (head_qk, head_v)
    return pl.pallas_call(
        functools.partial(_mla_proj_kernel, need_q=need_q, rotate=rotate),
        out_shape=out_shape,
        grid=(b, l // t),
        in_specs=in_specs,
        out_specs=out_specs,
        compiler_params=_params(2),
        name="mla_project_q" if need_q else "mla_project_kv",
    )(*ins)


def _flash_kernel(q_ref, kc_ref, vc_ref, kl_ref, vl_ref, o_ref, m_scr, l_scr, acc_scr):
    j = pl.program_id(3)

    @pl.when(j == 0)
    def _():
        m_scr[...] = jnp.full(m_scr.shape, -jnp.inf, F32)
        l_scr[...] = jnp.zeros(l_scr.shape, F32)
        acc_scr[...] = jnp.zeros(acc_scr.shape, F32)

    def update(k, v):
        s = _dot_nt(q_ref[...], k)
        m_prev = m_scr[...]
        m_new = jnp.maximum(m_prev, jnp.max(s, axis=-1, keepdims=True))
        alpha = jnp.exp(m_prev - m_new)
        p = jnp.exp(s - m_new)
        l_scr[...] = alpha * l_scr[...] + jnp.sum(p, axis=-1, keepdims=True)
        acc_scr[...] = alpha * acc_scr[...] + _dot(p.astype(v.dtype), v)
        m_scr[...] = m_new

    @pl.when(j == 0)
    def _():
        update(kc_ref[...], vc_ref[...])

    @pl.when(j > 0)
    def _():
        update(kl_ref[...], vl_ref[...])

    @pl.when(j == pl.num_programs(3) - 1)
    def _():
        o_ref[...] = acc_scr[...] / l_scr[...]


def _mla_attention(q, k_ctx, v_ctx, k_lat, v_lat):
    b, hh, l, dq = q.shape
    lc = k_ctx.shape[2]
    dv = v_lat.shape[-1]
    tq = _tile(l, 512)
    tk = _tile(l, 1024)
    return pl.pallas_call(
        _flash_kernel,
        out_shape=jax.ShapeDtypeStruct((b, l, hh * dv), F32),
        grid=(b, hh, l // tq, 1 + l // tk),
        in_specs=[pl.BlockSpec((None, None, tq, dq), lambda bi, h, qi, j: (bi, h, qi, 0)),
                  pl.BlockSpec((None, None, lc, dq), lambda bi, h, qi, j: (bi, h, 0, 0)),
                  pl.BlockSpec((None, None, lc, dv), lambda bi, h, qi, j: (bi, h, 0, 0)),
                  pl.BlockSpec((None, None, tk, dq), lambda bi, h, qi, j: (bi, h, jnp.maximum(j - 1, 0), 0)),
                  pl.BlockSpec((None, None, tk, dv), lambda bi, h, qi, j: (bi, h, jnp.maximum(j - 1, 0), 0))],
        out_specs=pl.BlockSpec((None, tq, dv), lambda bi, h, qi, j: (bi, qi, h)),
        scratch_shapes=[pltpu.VMEM((tq, 1), F32), pltpu.VMEM((tq, 1), F32), pltpu.VMEM((tq, dv), F32)],
        compiler_params=_params(4),
        name="mla_attention",
    )(q, k_ctx, v_ctx, k_lat, v_lat)


def _gated_out_kernel(x_ref, a_ref, z_ref, gate_ref, w_ref, o_ref):
    s = (a_ref[...] * _silu(z_ref[...])).astype(BF16)
    o_ref[...] = x_ref[...] + gate_ref[...] * _dot(s, w_ref[...])


def _gated_out(x, a, z, gate, w_out):
    b, l, d = x.shape
    e = a.shape[-1]
    t = _tile(l, 512)
    tok_d = pl.BlockSpec((None, t, d), lambda bi, ti: (bi, ti, 0))
    tok_e = pl.BlockSpec((None, t, e), lambda bi, ti: (bi, ti, 0))
    return pl.pallas_call(
        _gated_out_kernel,
        out_shape=jax.ShapeDtypeStruct(x.shape, F32),
        grid=(b, l // t),
        in_specs=[tok_d, tok_e, tok_e, _mod_spec(d), _const_spec(w_out.shape)],
        out_specs=tok_d,
        compiler_params=_params(2),
        name="mla_output",
    )(x, a, z, gate, w_out.astype(BF16))


def kernel(x, c, ctx, c_ctx, ada_w, ada_b, sgu_w_in, sgu_gain, sgu_w_s, sgu_b_s, sgu_w_out, rwkv_mu, rwkv_w_in, rwkv_w_lora1, rwkv_w_lora2, rwkv_w0, rwkv_a_lora1, rwkv_a_lora2, rwkv_a0, rwkv_k_k, rwkv_k_a, rwkv_r_k, rwkv_ln_gain, rwkv_ln_bias, rwkv_w_out, mla_w_in, mla_q_norm, mla_kv_norm, mla_w_uq, mla_w_ukv, mla_qk_gain_q, mla_qk_gain_k, mla_w_out):
    bsz, length, d = x.shape
    depth = ada_w.shape[0]
    n_cond = -(-(bsz + 1) // SUBLANES) * SUBLANES
    cond = jnp.concatenate([c, c_ctx[None], jnp.zeros((n_cond - bsz - 1, d), F32)])
    mods = _ada_modulation(cond, ada_w, ada_b)
    ctx_readers = [i for i in range(depth) if i % N_MIXERS != 0]
    last_ctx_reader = ctx_readers[-1] if ctx_readers else -1
    ta, tb = _rope_tables(length)
    for i in range(depth):
        kind, j = i % N_MIXERS, i // N_MIXERS
        update_ctx = i < last_ctx_reader
        shift, scale, gate = (mods[i, :bsz, s * d:(s + 1) * d][:, None, :] for s in range(3))
        c_shift, c_scale, c_gate = (jnp.broadcast_to(mods[i, bsz, s * d:(s + 1) * d], (bsz, 1, d)) for s in range(3))
        if kind == 0:
            sgu_args = (sgu_w_in[j], sgu_gain[j], sgu_w_s[j], sgu_b_s[j], sgu_w_out[j])
            x = _sgu_layer(x, shift, scale, gate, *sgu_args)
            if update_ctx:
                ctx = _sgu_layer(ctx, c_shift, c_scale, c_gate, *sgu_args)
        elif kind == 1:
            feat_args = (rwkv_mu[j], rwkv_w_in[j], rwkv_w_lora1[j], rwkv_w_lora2[j], rwkv_w0[j],
                         rwkv_a_lora1[j], rwkv_a_lora2[j], rwkv_a0[j], rwkv_k_k[j], rwkv_k_a[j],
                         rwkv_r_k[j].reshape(-1))
            out_args = (rwkv_ln_gain[j], rwkv_ln_bias[j], rwkv_w_out[j])
            r_c, v_c, kk_c, z_c, bonus_c, lw_c, kd_c, bb_c = _rwkv_features(ctx, c_shift, c_scale, *feat_args)
            state0 = jnp.zeros((2, bsz, r_c.shape[-1] // LANES, LANES, LANES), F32)
            y_c, s_ctx = _rwkv_scan(lw_c, kd_c, bb_c, v_c, kk_c, r_c, state0)
            r_l, v_l, kk_l, z_l, bonus_l, lw_l, kd_l, bb_l = _rwkv_features(x, shift, scale, *feat_args)
            y_l, _ = _rwkv_scan(lw_l, kd_l, bb_l, v_l, kk_l, r_l, s_ctx)
            x = _rwkv_output(x, y_l, bonus_l, z_l, gate, *out_args)
            if update_ctx:
                ctx = _rwkv_output(ctx, y_c, bonus_c, z_c, c_gate, *out_args)
        else:
            assert not update_ctx, "context-stream update after an MLA layer is not implemented"
            mla_args = (mla_w_in[j], mla_q_norm[j], mla_kv_norm[j], mla_w_uq[j], mla_w_ukv[j],
                        mla_qk_gain_q[j], mla_qk_gain_k[j])
            n_ctx = ctx.shape[1]
            k_c, v_c = _mla_project(ctx, c_shift, c_scale, *mla_args, ta[:n_ctx], tb[:n_ctx], need_q=False, rotate=False)
            q_l, k_l, v_l, z_l = _mla_project(x, shift, scale, *mla_args, ta, tb, need_q=True, rotate=True)
            o_l = _mla_attention(q_l, k_c, v_c, k_l, v_l)
            x = _gated_out(x, o_l, z_l, gate, mla_w_out[j])
    return x
```

```python
import functools
import math

import jax
import jax.numpy as jnp
from jax import lax
from jax.experimental import pallas as pl
from jax.experimental.pallas import tpu as pltpu

F32 = jnp.float32
BF16 = jnp.bfloat16
HIGHEST = lax.Precision.HIGHEST

N_MIXERS = 3
RMS_EPS = 1e-6
SGU_CHUNK = 128
SGU_GROUPS = 8
RWKV_HEAD = 64
RWKV_LN_EPS = 64e-5
RWKV_CHUNK = 64
MLA_HEADS = 16
MLA_NOPE = 128
MLA_ROPE = 64
MLA_VDIM = 128
MLA_Q_RANK = 768
MLA_KV_RANK = 256
MLA_QK_DIM = MLA_NOPE + MLA_ROPE
MLA_SCALE = MLA_QK_DIM ** -0.5
MLA_KV_TILE = 512
GRID_W = 64
ROPE_BASE = 10000.0

LANES = 128
SUBLANES = 8
VMEM_LIMIT = 56 * 1024 * 1024


def _params(n_axes):
    return pltpu.CompilerParams(dimension_semantics=("arbitrary",) * n_axes, vmem_limit_bytes=VMEM_LIMIT)


def _const_spec(shape):
    nd = len(shape)
    return pl.BlockSpec(shape, lambda *_: (0,) * nd, pipeline_mode=pl.Buffered(1))


def _dot(a, b, precision=None):
    return jnp.dot(a, b, preferred_element_type=F32, precision=precision)


def _dot_nt(a, b, precision=None):
    return lax.dot_general(a, b, (((1,), (1,)), ((), ())), preferred_element_type=F32, precision=precision)


def _dot_tn(a, b, precision=None):
    return lax.dot_general(a, b, (((0,), (0,)), ((), ())), preferred_element_type=F32, precision=precision)


def _bdot(a, b):
    return _dot(a.astype(BF16), b.astype(BF16))


def _sigmoid(x):
    return 1.0 / (1.0 + jnp.exp(-x))


def _silu(x):
    return x * _sigmoid(x)


def _gelu(x):
    return x * (0.5 * (1.0 + jnp.tanh(math.sqrt(2.0 / math.pi) * (x + 0.044715 * (x * x * x)))))


def _softplus(x):
    return jnp.maximum(x, 0.0) + jnp.log(1.0 + jnp.exp(-jnp.abs(x)))


def _rms(x, width=None):
    width = x.shape[-1] if width is None else width
    return x * lax.rsqrt(jnp.sum(x * x, axis=-1, keepdims=True) / width + RMS_EPS)


def _split_dot(x, w_bf16):
    hi = x.astype(BF16)
    lo = (x - hi.astype(F32)).astype(BF16)
    return _dot(hi, w_bf16) + _dot(lo, w_bf16)


def _head_sum(x, e_ref, et_ref):
    return _split_dot(_split_dot(x, e_ref[...]), et_ref[...])


def _head_indicator(width, head):
    e = (jnp.arange(width)[:, None] // head == jnp.arange(LANES)[None, :]).astype(BF16)
    return e, e.T


def _tile(length, pref):
    t = min(pref, length)
    assert length % t == 0, (length, t)
    return t


def _mod_spec(d):
    return pl.BlockSpec((None, 1, d), lambda b, t: (b, 0, 0))


def _ada_kernel(cond_ref, w_ref, b_ref, o_ref):
    c = cond_ref[...]
    o_ref[...] = _dot(_silu(c), w_ref[...], HIGHEST) + b_ref[...]


def _ada_modulation(cond, ada_w, ada_b):
    depth, d, d3 = ada_w.shape
    r = cond.shape[0]
    return pl.pallas_call(
        _ada_kernel,
        out_shape=jax.ShapeDtypeStruct((depth, r, d3), F32),
        grid=(depth, d3 // d),
        in_specs=[pl.BlockSpec((r, d), lambda i, j: (0, 0)),
                  pl.BlockSpec((None, d, d), lambda i, j: (i, 0, j)),
                  pl.BlockSpec((None, 1, d), lambda i, j: (i, 0, j))],
        out_specs=pl.BlockSpec((None, r, d), lambda i, j: (i, 0, j)),
        compiler_params=_params(2),
        name="ada_modulation",
    )(cond, ada_w, ada_b.reshape(depth, 1, d3))


def _sgu_kernel(x_ref, shift_ref, scale_ref, gate_ref, w_in_ref, gain_ref, w_s_ref, bias_ref, w_out_ref,
                o_ref, v_scr, s_scr):
    t = x_ref.shape[0]
    w = gain_ref.shape[-1]
    gw = w // SGU_GROUPS
    x = x_ref[...]
    hb = (_rms(x) * (1.0 + scale_ref[...]) + shift_ref[...]).astype(BF16)
    ssq = jnp.zeros((t, 1), F32)
    for g in range(SGU_GROUPS):
        vg = _gelu(_dot(hb, w_in_ref[:, w + g * gw:w + (g + 1) * gw]))
        v_scr[:, g * gw:(g + 1) * gw] = vg
        ssq = ssq + jnp.sum(vg * vg, axis=-1, keepdims=True)
    rs = lax.rsqrt(ssq / w + RMS_EPS)
    for g in range(SGU_GROUPS):
        cols = slice(g * gw, (g + 1) * gw)
        vn = (v_scr[:, cols] * rs * gain_ref[:, cols]).astype(BF16)
        u = _gelu(_dot(hb, w_in_ref[:, cols]))
        sz = _silu(_dot(hb, w_in_ref[:, 2 * w + g * gw:2 * w + (g + 1) * gw]))
        for c in range(t // SGU_CHUNK):
            rows = slice(c * SGU_CHUNK, (c + 1) * SGU_CHUNK)
            mixed = _dot(w_s_ref[g], vn[rows]) + bias_ref[:, cols]
            s_scr[rows, cols] = ((u[rows] * mixed) * sz[rows]).astype(BF16)
    o_ref[...] = x + gate_ref[...] * _dot(s_scr[...], w_out_ref[...])


def _sgu_layer(x, shift, scale, gate, w_in, gain, w_s, b_s, w_out):
    b, l, d = x.shape
    w = gain.shape[-1]
    t = _tile(l, 512)
    assert t % SGU_CHUNK == 0 and w % (SGU_GROUPS * LANES) == 0
    bias = jnp.repeat(b_s.T, w // SGU_GROUPS, axis=1)
    tok = pl.BlockSpec((None, t, d), lambda bi, ti: (bi, ti, 0))
    return pl.pallas_call(
        _sgu_kernel,
        out_shape=jax.ShapeDtypeStruct(x.shape, F32),
        grid=(b, l // t),
        in_specs=[tok, _mod_spec(d), _mod_spec(d), _mod_spec(d),
                  _const_spec((d, 3 * w)), _const_spec((1, w)), _const_spec(w_s.shape),
                  _const_spec(bias.shape), _const_spec((w, d))],
        out_specs=tok,
        scratch_shapes=[pltpu.VMEM((t, w), F32), pltpu.VMEM((t, w), BF16)],
        compiler_params=_params(2),
        name="sgu_layer",
    )(x, shift, scale, gate, w_in.astype(BF16), gain.reshape(1, w), w_s.astype(BF16), bias, w_out.astype(BF16))


def _rwkv_feat_kernel(x_ref, xp_ref, xn_ref, shift_ref, scale_ref, mu_ref, w_in_ref, wl1_ref, wl2_ref, w0_ref,
                      al1_ref, al2_ref, a0_ref, kk_w_ref, ka_ref, rk_ref, e_ref, et_ref,
                      r_o, v_o, kk_o, z_o, bonus_o, lw_o, kd_o, bb_o):
    t = x_ref.shape[0]
    ti = pl.program_id(1)
    nt = pl.num_programs(1)
    sc = 1.0 + scale_ref[...]
    sh = shift_ref[...]
    h = _rms(x_ref[...]) * sc + sh
    hp = (_rms(xp_ref[SUBLANES - 1:SUBLANES, :]) * sc + sh) * (ti > 0).astype(F32)
    hn = (_rms(xn_ref[0:1, :]) * sc + sh) * (ti < nt - 1).astype(F32)
    row = lax.broadcasted_iota(jnp.int32, (t, 1), 0)
    prev = jnp.where(row == 0, hp, pltpu.roll(h, 1, 0))
    nxt = jnp.where(row == t - 1, hn, pltpu.roll(h, t - 1, 0))
    xx = 0.5 * (prev + nxt) - h

    def lerp(c):
        return (h + xx * mu_ref[c:c + 1, :]).astype(BF16)

    r = _dot(lerp(0), w_in_ref[0])
    k = _dot(lerp(1), w_in_ref[1])
    v = _dot(lerp(2), w_in_ref[2])
    z_o[...] = _dot(lerp(3), w_in_ref[3])
    r_o[...] = r
    v_o[...] = v
    kkr = k * kk_w_ref[...]
    kkn = kkr * _split_dot(lax.rsqrt(_split_dot(kkr * kkr, e_ref[...]) + 1e-12), et_ref[...])
    kk_o[...] = kkn
    t_w = jnp.tanh(_dot(lerp(4), wl1_ref[...])).astype(BF16)
    t_a = _dot(lerp(5), al1_ref[...]).astype(BF16)
    kd_sum = jnp.zeros_like(k)
    for n in range(2):
        log_w = -_softplus(-(w0_ref[n:n + 1, :] + _dot(t_w, wl2_ref[n]))) - 0.5
        lw_o[n] = -jnp.exp(log_w)
        a = _sigmoid(a0_ref[n:n + 1, :] + _dot(t_a, al2_ref[n]))
        kd = k * (1.0 + (a - 1.0) * ka_ref[...])
        kd_o[n] = kd
        bb_o[n] = kkn * a
        kd_sum = kd_sum + kd
    bonus_o[...] = _head_sum(r * kd_sum * rk_ref[...], e_ref, et_ref) * v


def _rwkv_features(x, shift, scale, mu, w_in, w_lora1, w_lora2, w0, a_lora1, a_lora2, a0, k_k, k_a, r_k):
    b, l, d = x.shape
    e = w_in.shape[-1]
    t = _tile(l, 256)
    nb8 = l // SUBLANES
    tok = pl.BlockSpec((None, t, d), lambda bi, ti: (bi, ti, 0))
    prev8 = pl.BlockSpec((None, SUBLANES, d), lambda bi, ti: (bi, jnp.maximum(ti * (t // SUBLANES) - 1, 0), 0))
    next8 = pl.BlockSpec((None, SUBLANES, d), lambda bi, ti: (bi, jnp.minimum((ti + 1) * (t // SUBLANES), nb8 - 1), 0))
    rank = w_lora1.shape[-1]
    assert 2 * rank == LANES and a_lora1.shape[-1] == rank
    zr = jnp.zeros((rank, e), F32)
    wl1 = jnp.concatenate([w_lora1[0], w_lora1[1]], axis=1).astype(BF16)
    al1 = jnp.concatenate([a_lora1[0], a_lora1[1]], axis=1).astype(BF16)
    wl2 = jnp.stack([jnp.concatenate([w_lora2[0], zr]), jnp.concatenate([zr, w_lora2[1]])]).astype(BF16)
    al2 = jnp.stack([jnp.concatenate([a_lora2[0], zr]), jnp.concatenate([zr, a_lora2[1]])]).astype(BF16)
    ind, ind_t = _head_indicator(e, RWKV_HEAD)
    out_tok = pl.BlockSpec((None, t, e), lambda bi, ti: (bi, ti, 0))
    out_dir = pl.BlockSpec((2, None, t, e), lambda bi, ti: (0, bi, ti, 0))
    shared = jax.ShapeDtypeStruct((b, l, e), F32)
    per_dir = jax.ShapeDtypeStruct((2, b, l, e), F32)
    return pl.pallas_call(
        _rwkv_feat_kernel,
        out_shape=(shared,) * 5 + (per_dir,) * 3,
        grid=(b, l // t),
        in_specs=[tok, prev8, next8, _mod_spec(d), _mod_spec(d),
                  _const_spec(mu.shape), _const_spec(w_in.shape), _const_spec(wl1.shape), _const_spec(wl2.shape),
                  _const_spec(w0.shape), _const_spec(al1.shape), _const_spec(al2.shape), _const_spec(a0.shape),
                  _const_spec((1, e)), _const_spec((1, e)), _const_spec((1, e)),
                  _const_spec(ind.shape), _const_spec(ind_t.shape)],
        out_specs=(out_tok,) * 5 + (out_dir,) * 3,
        compiler_params=_params(2),
        name="rwkv_features",
    )(x, x, x, shift, scale, mu, w_in.astype(BF16), wl1, wl2, w0, al1, al2, a0,
      k_k.reshape(1, e), k_a.reshape(1, e), r_k.reshape(1, e), ind, ind_t)


def _unit_triangular_inverse(a, eye, r2, c2, size):
    n = range(len(a))
    blk = (r2 >> 3) == (c2 >> 3)
    d = [jnp.where(blk, a[i], 0.0) for i in n]
    d_b = [d[i].astype(BF16) for i in n]
    d2 = [_dot(d_b[i], d_b[i]) for i in n]
    d2_b = [d2[i].astype(BF16) for i in n]
    d4 = [_dot(d2_b[i], d2_b[i]) for i in n]
    inv = [_bdot(eye - d[i], eye + d2[i]) for i in n]
    inv = [_bdot(inv[i], eye + d4[i]) for i in n]
    s = 8
    while s < size:
        sh = s.bit_length() - 1
        off = ((r2 >> (sh + 1)) == (c2 >> (sh + 1))) & ((r2 >> sh) != (c2 >> sh))
        inv_b = [inv[i].astype(BF16) for i in n]
        e_inv = [_dot(jnp.where(off, a[i], 0.0).astype(BF16), inv_b[i]).astype(BF16) for i in n]
        inv = [inv[i] - _dot(inv_b[i], e_inv[i]) for i in n]
        s *= 2
    return inv


def _rwkv_scan_kernel(lw_ref, kd_ref, bb_ref, v_ref, kk_ref, r_ref, s0_ref, y_ref, sf_ref, st_scr):
    c, hn = v_ref.shape
    half = LANES // 2
    assert c == half == RWKV_HEAD
    ci = pl.program_id(2)
    nc = pl.num_programs(2)

    @pl.when(ci == 0)
    def _():
        st_scr[...] = s0_ref[...]

    sgn = jnp.where(pl.program_id(0) == 0, 1, -1)
    rc = lax.broadcasted_iota(jnp.int32, (c, c), 0)
    cc = lax.broadcasted_iota(jnp.int32, (c, c), 1)
    cum = (sgn * (rc - cc) >= 0).astype(F32)
    lw = lw_ref[...]
    cin = _dot(cum, lw, HIGHEST)
    ctot = _dot_tn(lw, jnp.ones((c, LANES), F32), HIGHEST)
    p_in = jnp.exp(cin)
    inv_in = jnp.exp(-cin)
    p_rest = jnp.exp(jnp.sum(lw, axis=0, keepdims=True) - cin)
    rt = r_ref[...] * p_in
    kt = kk_ref[...] * jnp.exp(cin - lw)
    kbar = kd_ref[...] * inv_in
    bbar = bb_ref[...] * inv_in
    khat = kd_ref[...] * p_rest
    bhat = bb_ref[...] * p_rest
    v = v_ref[...]

    n2 = 2 * c
    r2 = lax.broadcasted_iota(jnp.int32, (n2, n2), 0)
    c2 = lax.broadcasted_iota(jnp.int32, (n2, n2), 1)
    tdiff = sgn * ((r2 & (c - 1)) - (c2 & (c - 1)))
    strict = tdiff > 0
    incl = tdiff >= 0
    eye = (r2 == c2).astype(F32)
    first = lax.broadcasted_iota(jnp.int32, (1, LANES), 1) < half

    def stack(xp):
        return jnp.concatenate([jnp.where(first, xp, 0.0), jnp.where(first, 0.0, xp)], axis=0)

    pairs = range(hn // LANES)
    sls = [slice(p * LANES, (p + 1) * LANES) for p in pairs]
    v_s = [stack(v[:, sl]).astype(BF16) for sl in sls]
    lhs = [jnp.concatenate([stack(kt[:, sl]), stack(rt[:, sl])], axis=0).astype(BF16) for sl in sls]
    gb = [_dot_nt(lhs[p], stack(bbar[:, sls[p]]).astype(BF16)) for p in pairs]
    gk = [_dot_nt(lhs[p], stack(kbar[:, sls[p]]).astype(BF16)) for p in pairs]
    states = [st_scr[p] for p in pairs]
    m1 = [_dot(lhs[p], states[p].astype(BF16)) for p in pairs]
    a_m = [jnp.where(strict, gb[p][:n2], 0.0) for p in pairs]
    qb = [jnp.where(incl, gb[p][n2:], 0.0) for p in pairs]
    b_m = [jnp.where(strict, gk[p][:n2], 0.0).astype(BF16) for p in pairs]
    qk = [jnp.where(incl, gk[p][n2:], 0.0) for p in pairs]
    rhs = [m1[p][:n2] + _dot(b_m[p], v_s[p]) for p in pairs]
    t_inv = _unit_triangular_inverse(a_m, eye, r2, c2, c)
    u = [_bdot(t_inv[p], rhs[p]) for p in pairs]
    vu = [jnp.concatenate([v_s[p], u[p].astype(BF16)], axis=0) for p in pairs]
    for p in pairs:
        y = m1[p][n2:] + _dot(jnp.concatenate([qk[p], -qb[p]], axis=1).astype(BF16), vu[p])
        y_ref[:, sls[p]] = y[:c] + y[c:]
    for p in pairs:
        x_hat = jnp.concatenate([stack(khat[:, sls[p]]), -stack(bhat[:, sls[p]])], axis=0).astype(BF16)
        st_scr[p] = jnp.exp(ctot[sls[p]]) * states[p] + _dot_tn(x_hat, vu[p])

    @pl.when(ci == nc - 1)
    def _():
        sf_ref[...] = st_scr[...]


def _rwkv_scan(lw, kd, bb, v, kk, r, s0):
    _, b, l, hn = lw.shape
    c = RWKV_CHUNK
    nc = l // c
    npair = hn // LANES
    assert l % c == 0 and hn % LANES == 0

    def chunk(d, ci):
        return ci + d * (nc - 1 - 2 * ci)

    per_dir = pl.BlockSpec((None, None, c, hn), lambda d, bi, ci: (d, bi, chunk(d, ci), 0))
    shared = pl.BlockSpec((None, c, hn), lambda d, bi, ci: (bi, chunk(d, ci), 0))
    st = pl.BlockSpec((None, None, npair, LANES, LANES), lambda d, bi, ci: (d, bi, 0, 0, 0))
    return pl.pallas_call(
        _rwkv_scan_kernel,
        out_shape=(jax.ShapeDtypeStruct((2, b, l, hn), F32), jax.ShapeDtypeStruct(s0.shape, F32)),
        grid=(2, b, nc),
        in_specs=[per_dir, per_dir, per_dir, shared, shared, shared, st],
        out_specs=(per_dir, st),
        scratch_shapes=[pltpu.VMEM((npair, LANES, LANES), F32)],
        compiler_params=_params(3),
        name="rwkv_scan",
    )(lw, kd, bb, v, kk, r, s0)


def _rwkv_out_kernel(x_ref, y_ref, bonus_ref, z_ref, gate_ref, lng_ref, lnb_ref, e_ref, et_ref, w_out_ref, o_ref):
    y = y_ref[0] + y_ref[1]
    mean = _head_sum(y, e_ref, et_ref) * (1.0 / RWKV_HEAD)
    yc = y - mean
    var = _head_sum(yc * yc, e_ref, et_ref) * (1.0 / RWKV_HEAD)
    yn = yc * lax.rsqrt(var + RWKV_LN_EPS) * lng_ref[...] + lnb_ref[...]
    out = (yn + bonus_ref[...]) * _silu(z_ref[...])
    o_ref[...] = x_ref[...] + gate_ref[...] * _dot(out.astype(BF16), w_out_ref[...])


def _rwkv_output(x, y, bonus, z, gate, ln_gain, ln_bias, w_out):
    b, l, d = x.shape
    e = y.shape[-1]
    t = _tile(l, 512)
    ind, ind_t = _head_indicator(e, RWKV_HEAD)
    tok_d = pl.BlockSpec((None, t, d), lambda bi, ti: (bi, ti, 0))
    tok_e = pl.BlockSpec((None, t, e), lambda bi, ti: (bi, ti, 0))
    return pl.pallas_call(
        _rwkv_out_kernel,
        out_shape=jax.ShapeDtypeStruct(x.shape, F32),
        grid=(b, l // t),
        in_specs=[tok_d, pl.BlockSpec((2, None, t, e), lambda bi, ti: (0, bi, ti, 0)), tok_e, tok_e, _mod_spec(d),
                  _const_spec((1, e)), _const_spec((1, e)), _const_spec(ind.shape), _const_spec(ind_t.shape),
                  _const_spec(w_out.shape)],
        out_specs=tok_d,
        compiler_params=_params(2),
        name="rwkv_output",
    )(x, y, bonus, z, gate, ln_gain.reshape(1, e), ln_bias.reshape(1, e), ind, ind_t, w_out.astype(BF16))


def _mla_proj_kernel(*refs, need_q, rotate):
    if need_q:
        (x_ref, shift_ref, scale_ref, w_in_ref, qn_ref, kvn_ref, w_uq_ref, w_uk_ref, w_uvt_ref, gq_ref, gk_ref,
         ta_ref, tb_ref, q_o, k_o, vt_o, z_o) = refs
    else:
        (x_ref, shift_ref, scale_ref, w_in_ref, kvn_ref, w_uk_ref, w_uvt_ref, gk_ref, ta_ref, tb_ref, k_o, vt_o) = refs
    hb = (_rms(x_ref[...]) * (1.0 + scale_ref[...]) + shift_ref[...]).astype(BF16)
    o_kv = MLA_Q_RANK if need_q else 0
    o_kr = o_kv + MLA_KV_RANK
    first = lax.broadcasted_iota(jnp.int32, (1, LANES), 1) < MLA_ROPE
    ta = ta_ref[...]
    tb = tb_ref[...]

    def rope_tile(t2, gain_ref, use_rot, use_plain):
        sw = pltpu.roll(t2, MLA_ROPE, 1)
        g_a = gain_ref[1:2, :]
        g_b = gain_ref[2:3, :]
        m_b = tb * g_b
        if not use_rot:
            return sw * jnp.where(first, 0.0, m_b)
        if not use_plain:
            m_b = jnp.where(first, m_b, 0.0)
        return t2 * (ta * g_a) + sw * m_b

    ckv = (_rms(_dot(hb, w_in_ref[:, o_kv:o_kr])) * kvn_ref[...]).astype(BF16)
    vt = _dot_nt(w_uvt_ref[...], ckv)
    kr = _dot(hb, w_in_ref[:, o_kr:o_kr + LANES])
    kr_ssq = jnp.sum(jnp.where(first, kr * kr, 0.0), axis=-1, keepdims=True)
    kr_out = rope_tile(kr, gk_ref, rotate, not rotate)
    for h in range(MLA_HEADS):
        vt_o[h] = vt[h * MLA_VDIM:(h + 1) * MLA_VDIM].astype(vt_o.dtype)
        kn = _dot(ckv, w_uk_ref[:, h * LANES:(h + 1) * LANES])
        rs = lax.rsqrt((jnp.sum(kn * kn, axis=-1, keepdims=True) + kr_ssq) / MLA_QK_DIM + RMS_EPS)
        k_o[h] = jnp.concatenate([kn * rs * gk_ref[0:1, :], kr_out * rs], axis=1).astype(k_o.dtype)
    if need_q:
        z_o[...] = _dot(hb, w_in_ref[:, o_kr + LANES:])
        cq = (_rms(_dot(hb, w_in_ref[:, :MLA_Q_RANK])) * qn_ref[...]).astype(BF16)
        for h in range(MLA_HEADS):
            t1 = _dot(cq, w_uq_ref[:, h * 2 * LANES:h * 2 * LANES + LANES])
            t2 = _dot(cq, w_uq_ref[:, h * 2 * LANES + LANES:(h + 1) * 2 * LANES])
            ssq = jnp.sum(t1 * t1, axis=-1, keepdims=True) + jnp.sum(jnp.where(first, t2 * t2, 0.0), axis=-1, keepdims=True)
            rs = lax.rsqrt(ssq / MLA_QK_DIM + RMS_EPS) * (MLA_SCALE * math.log2(math.e))
            q_o[h] = jnp.concatenate([t1 * rs * gq_ref[0:1, :], rope_tile(t2, gq_ref, True, True) * rs],
                                     axis=1).astype(q_o.dtype)


def _swap_halves(w):
    half = w.shape[-1] // 2
    return jnp.concatenate([w[..., half:], w[..., :half]], axis=-1)


def _qk_gain_rows(g):
    gn, gr = g[:MLA_NOPE], g[MLA_NOPE:]
    zero = jnp.zeros_like(gr)
    rows = jnp.stack([gn, jnp.concatenate([gr, zero]), jnp.concatenate([_swap_halves(gr), gr])])
    return jnp.concatenate([rows, jnp.zeros((SUBLANES - 3, LANES), F32)])


def _rope_tables(length):
    rows = length // GRID_W
    row = jnp.repeat(jnp.arange(rows, dtype=F32), GRID_W)
    col = jnp.tile(jnp.arange(GRID_W, dtype=F32), rows)
    nf = MLA_ROPE // 4
    inv_freq = ROPE_BASE ** (-jnp.arange(nf, dtype=F32) / nf)
    ang = jnp.concatenate([row[:, None] * inv_freq, col[:, None] * inv_freq], axis=-1)
    cos, sin = jnp.cos(ang), jnp.sin(ang)
    ta = jnp.concatenate([cos, cos, jnp.zeros((length, MLA_ROPE), F32)], axis=-1)
    tb = jnp.concatenate([-sin, sin, jnp.ones((length, MLA_ROPE), F32)], axis=-1)
    return ta, tb


def _mla_project(x, shift, scale, w_in, q_norm, kv_norm, w_uq, w_ukv, gq, gk, ta, tb, need_q, rotate):
    b, l, d = x.shape
    hh = MLA_HEADS
    t = _tile(l, MLA_KV_TILE)
    o1, o2, o3 = MLA_Q_RANK, MLA_Q_RANK + MLA_KV_RANK, MLA_Q_RANK + MLA_KV_RANK + MLA_ROPE
    w_kr = w_in[:, o2:o3]
    cols = ([w_in[:, :o1]] if need_q else []) + [w_in[:, o1:o2], w_kr, _swap_halves(w_kr)] + ([w_in[:, o3:]] if need_q else [])
    w_in_p = jnp.concatenate(cols, axis=1).astype(BF16)
    tok = pl.BlockSpec((None, t, d), lambda bi, ti: (bi, ti, 0))
    tab = pl.BlockSpec((t, LANES), lambda bi, ti: (ti, 0))
    head_qk = pl.BlockSpec((None, hh, t, 2 * LANES), lambda bi, ti: (bi, 0, ti, 0))
    head_vt = pl.BlockSpec((None, hh, None, MLA_VDIM, t), lambda bi, ti: (bi, 0, ti, 0, 0))
    qk_shape = jax.ShapeDtypeStruct((b, hh, l, 2 * LANES), BF16)
    vt_shape = jax.ShapeDtypeStruct((b, hh, l // t, MLA_VDIM, t), BF16)
    kvn = kv_norm.reshape(1, -1)
    wkv = w_ukv.reshape(MLA_KV_RANK, hh, MLA_NOPE + MLA_VDIM)
    w_uk = wkv[..., :MLA_NOPE].reshape(MLA_KV_RANK, hh * MLA_NOPE).astype(BF16)
    w_uvt = wkv[..., MLA_NOPE:].reshape(MLA_KV_RANK, hh * MLA_VDIM).T.astype(BF16)
    gk_rows = _qk_gain_rows(gk)
    if need_q:
        wq = w_uq.reshape(MLA_Q_RANK, hh, MLA_QK_DIM)
        w_uq_p = jnp.concatenate([wq, _swap_halves(wq[..., MLA_NOPE:])], axis=-1).reshape(MLA_Q_RANK, hh * 2 * LANES)
        width = w_in.shape[1] - o3
        ins = (x, shift, scale, w_in_p, q_norm.reshape(1, -1), kvn, w_uq_p.astype(BF16), w_uk, w_uvt,
               _qk_gain_rows(gq), gk_rows, ta, tb)
        in_specs = [tok, _mod_spec(d), _mod_spec(d), _const_spec(w_in_p.shape), _const_spec((1, MLA_Q_RANK)),
                    _const_spec(kvn.shape), _const_spec(w_uq_p.shape), _const_spec(w_uk.shape), _const_spec(w_uvt.shape),
                    _const_spec((SUBLANES, LANES)), _const_spec((SUBLANES, LANES)), tab, tab]
        out_shape = (qk_shape, qk_shape, vt_shape, jax.ShapeDtypeStruct((b, l, width), F32))
        out_specs = (head_qk, head_qk, head_vt, pl.BlockSpec((None, t, width), lambda bi, ti: (bi, ti, 0)))
    else:
        ins = (x, shift, scale, w_in_p, kvn, w_uk, w_uvt, gk_rows, ta, tb)
        in_specs = [tok, _mod_spec(d), _mod_spec(d), _const_spec(w_in_p.shape), _const_spec(kvn.shape),
                    _const_spec(w_uk.shape), _const_spec(w_uvt.shape), _const_spec((SUBLANES, LANES)), tab, tab]
        out_shape = (qk_shape, vt_shape)
        out_specs = (head_qk, head_vt)
    return pl.pallas_call(
        functools.partial(_mla_proj_kernel, need_q=need_q, rotate=rotate),
        out_shape=out_shape,
        grid=(b, l // t),
        in_specs=in_specs,
        out_specs=out_specs,
        compiler_params=_params(2),
        name="mla_project_q" if need_q else "mla_project_kv",
    )(*ins)


def _flash_kernel(q_ref, kc_ref, vc_ref, kl_ref, vl_ref, o_ref, m_scr, l_scr, acc_scr, sa_scr, sb_scr):
    nkv, _, tk = vl_ref.shape
    q = q_ref[...]

    def scores(j):
        return _dot_nt(kl_ref[pl.ds(pl.multiple_of(j * tk, tk), tk), :], q)

    def accumulate(s, vt):
        m_prev = m_scr[...]
        m_new = jnp.maximum(m_prev, jnp.max(s, axis=0, keepdims=True))
        alpha = jnp.exp2(m_prev - m_new)
        p = jnp.exp2(s - m_new)
        l_scr[...] = alpha * l_scr[...] + jnp.sum(p, axis=0, keepdims=True)
        acc_scr[...] = alpha * acc_scr[...] + _dot(vt, p.astype(BF16))
        m_scr[...] = m_new

    sa_scr[...] = scores(0)
    st = _dot_nt(kc_ref[...], q)
    m0 = jnp.max(st, axis=0, keepdims=True)
    p0 = jnp.exp2(st - m0)
    m_scr[...] = m0
    l_scr[...] = jnp.sum(p0, axis=0, keepdims=True)
    acc_scr[...] = _dot(vc_ref[0], p0.astype(BF16))

    def body(i, carry):
        j = 2 * i
        sb_scr[...] = scores(j + 1)
        accumulate(sa_scr[...], vl_ref[j])
        sa_scr[...] = scores(jnp.minimum(j + 2, nkv - 1))
        accumulate(sb_scr[...], vl_ref[j + 1])
        return carry

    if nkv // 2:
        lax.fori_loop(0, nkv // 2, body, 0)
    if nkv % 2:
        accumulate(sa_scr[...], vl_ref[nkv - 1])
    o_ref[...] = jnp.transpose(acc_scr[...] / l_scr[...])


def _mla_attention(q, k_ctx, vt_ctx, k_lat, vt_lat):
    b, hh, l, dq = q.shape
    lc = k_ctx.shape[2]
    _, _, nkv, dv, tk = vt_lat.shape
    tq = _tile(l, 1024)
    return pl.pallas_call(
        _flash_kernel,
        out_shape=jax.ShapeDtypeStruct((b, l, hh * dv), F32),
        grid=(b, hh, l // tq),
        in_specs=[pl.BlockSpec((None, None, tq, dq), lambda bi, h, qi: (bi, h, qi, 0)),
                  pl.BlockSpec((None, None, lc, dq), lambda bi, h, qi: (bi, h, 0, 0)),
                  pl.BlockSpec((None, None, 1, dv, lc), lambda bi, h, qi: (bi, h, 0, 0, 0)),
                  pl.BlockSpec((None, None, l, dq), lambda bi, h, qi: (bi, h, 0, 0)),
                  pl.BlockSpec((None, None, nkv, dv, tk), lambda bi, h, qi: (bi, h, 0, 0, 0))],
        out_specs=pl.BlockSpec((None, tq, dv), lambda bi, h, qi: (bi, qi, h)),
        scratch_shapes=[pltpu.VMEM((1, tq), F32), pltpu.VMEM((1, tq), F32), pltpu.VMEM((dv, tq), F32),
                        pltpu.VMEM((tk, tq), F32), pltpu.VMEM((tk, tq), F32)],
        compiler_params=_params(3),
        name="mla_attention",
    )(q, k_ctx, vt_ctx, k_lat, vt_lat)


def _gated_out_kernel(x_ref, a_ref, z_ref, gate_ref, w_ref, o_ref):
    s = (a_ref[...] * _silu(z_ref[...])).astype(BF16)
    o_ref[...] = x_ref[...] + gate_ref[...] * _dot(s, w_ref[...])


def _gated_out(x, a, z, gate, w_out):
    b, l, d = x.shape
    e = a.shape[-1]
    t = _tile(l, 512)
    tok_d = pl.BlockSpec((None, t, d), lambda bi, ti: (bi, ti, 0))
    tok_e = pl.BlockSpec((None, t, e), lambda bi, ti: (bi, ti, 0))
    return pl.pallas_call(
        _gated_out_kernel,
        out_shape=jax.ShapeDtypeStruct(x.shape, F32),
        grid=(b, l // t),
        in_specs=[tok_d, tok_e, tok_e, _mod_spec(d), _const_spec(w_out.shape)],
        out_specs=tok_d,
        compiler_params=_params(2),
        name="mla_output",
    )(x, a, z, gate, w_out.astype(BF16))


def kernel(x, c, ctx, c_ctx, ada_w, ada_b, sgu_w_in, sgu_gain, sgu_w_s, sgu_b_s, sgu_w_out, rwkv_mu, rwkv_w_in, rwkv_w_lora1, rwkv_w_lora2, rwkv_w0, rwkv_a_lora1, rwkv_a_lora2, rwkv_a0, rwkv_k_k, rwkv_k_a, rwkv_r_k, rwkv_ln_gain, rwkv_ln_bias, rwkv_w_out, mla_w_in, mla_q_norm, mla_kv_norm, mla_w_uq, mla_w_ukv, mla_qk_gain_q, mla_qk_gain_k, mla_w_out):
    bsz, length, d = x.shape
    depth = ada_w.shape[0]
    n_cond = -(-(bsz + 1) // SUBLANES) * SUBLANES
    cond = jnp.concatenate([c, c_ctx[None], jnp.zeros((n_cond - bsz - 1, d), F32)])
    mods = _ada_modulation(cond, ada_w, ada_b)
    ctx_readers = [i for i in range(depth) if i % N_MIXERS != 0]
    last_ctx_reader = ctx_readers[-1] if ctx_readers else -1
    ta, tb = _rope_tables(length)
    for i in range(depth):
        kind, j = i % N_MIXERS, i // N_MIXERS
        update_ctx = i < last_ctx_reader
        shift, scale, gate = (mods[i, :bsz, s * d:(s + 1) * d][:, None, :] for s in range(3))
        c_shift, c_scale, c_gate = (jnp.broadcast_to(mods[i, bsz, s * d:(s + 1) * d], (bsz, 1, d)) for s in range(3))
        if kind == 0:
            sgu_args = (sgu_w_in[j], sgu_gain[j], sgu_w_s[j], sgu_b_s[j], sgu_w_out[j])
            x = _sgu_layer(x, shift, scale, gate, *sgu_args)
            if update_ctx:
                ctx = _sgu_layer(ctx, c_shift, c_scale, c_gate, *sgu_args)
        elif kind == 1:
            feat_args = (rwkv_mu[j], rwkv_w_in[j], rwkv_w_lora1[j], rwkv_w_lora2[j], rwkv_w0[j],
                         rwkv_a_lora1[j], rwkv_a_lora2[j], rwkv_a0[j], rwkv_k_k[j], rwkv_k_a[j],
                         rwkv_r_k[j].reshape(-1))
            out_args = (rwkv_ln_gain[j], rwkv_ln_bias[j], rwkv_w_out[j])
            r_c, v_c, kk_c, z_c, bonus_c, lw_c, kd_c, bb_c = _rwkv_features(ctx, c_shift, c_scale, *feat_args)
            state0 = jnp.zeros((2, bsz, r_c.shape[-1] // LANES, LANES, LANES), F32)
            y_c, s_ctx = _rwkv_scan(lw_c, kd_c, bb_c, v_c, kk_c, r_c, state0)
            r_l, v_l, kk_l, z_l, bonus_l, lw_l, kd_l, bb_l = _rwkv_features(x, shift, scale, *feat_args)
            y_l, _ = _rwkv_scan(lw_l, kd_l, bb_l, v_l, kk_l, r_l, s_ctx)
            x = _rwkv_output(x, y_l, bonus_l, z_l, gate, *out_args)
            if update_ctx:
                ctx = _rwkv_output(ctx, y_c, bonus_c, z_c, c_gate, *out_args)
        else:
            assert not update_ctx, "context-stream update after an MLA layer is not implemented"
            mla_args = (mla_w_in[j], mla_q_norm[j], mla_kv_norm[j], mla_w_uq[j], mla_w_ukv[j],
                        mla_qk_gain_q[j], mla_qk_gain_k[j])
            n_ctx = ctx.shape[1]
            k_c, v_c = _mla_project(ctx, c_shift, c_scale, *mla_args, ta[:n_ctx], tb[:n_ctx], need_q=False, rotate=False)
            q_l, k_l, v_l, z_l = _mla_project(x, shift, scale, *mla_args, ta, tb, need_q=True, rotate=True)
            o_l = _mla_attention(q_l, k_c, v_c, k_l, v_l)
            x = _gated_out(x, o_l, z_l, gate, mla_w_out[j])
    return x
```

```python
import functools
import math

import jax
import jax.numpy as jnp
from jax import lax
from jax.experimental import pallas as pl
from jax.experimental.pallas import tpu as pltpu

F32 = jnp.float32
BF16 = jnp.bfloat16
HIGHEST = lax.Precision.HIGHEST

N_MIXERS = 3
RMS_EPS = 1e-6
SGU_CHUNK = 128
SGU_GROUPS = 8
RWKV_HEAD = 64
RWKV_LN_EPS = 64e-5
RWKV_CHUNK = 64
MLA_HEADS = 16
MLA_NOPE = 128
MLA_ROPE = 64
MLA_VDIM = 128
MLA_Q_RANK = 768
MLA_KV_RANK = 256
MLA_QK_DIM = MLA_NOPE + MLA_ROPE
MLA_SCALE = MLA_QK_DIM ** -0.5
MLA_KV_TILE = 512
MLA_VT_ROWS = MLA_VDIM + 16
GRID_W = 64
ROPE_BASE = 10000.0

LANES = 128
SUBLANES = 8
VMEM_LIMIT = 56 * 1024 * 1024


def _params(n_axes):
    return pltpu.CompilerParams(dimension_semantics=("arbitrary",) * n_axes, vmem_limit_bytes=VMEM_LIMIT)


def _const_spec(shape):
    nd = len(shape)
    return pl.BlockSpec(shape, lambda *_: (0,) * nd, pipeline_mode=pl.Buffered(1))


def _dot(a, b, precision=None):
    return jnp.dot(a, b, preferred_element_type=F32, precision=precision)


def _dot_nt(a, b, precision=None):
    return lax.dot_general(a, b, (((1,), (1,)), ((), ())), preferred_element_type=F32, precision=precision)


def _dot_tn(a, b, precision=None):
    return lax.dot_general(a, b, (((0,), (0,)), ((), ())), preferred_element_type=F32, precision=precision)


def _bdot(a, b):
    return _dot(a.astype(BF16), b.astype(BF16))


def _sigmoid(x):
    return 1.0 / (1.0 + jnp.exp(-x))


def _silu(x):
    return x * _sigmoid(x)


def _gelu(x):
    return x * (0.5 * (1.0 + jnp.tanh(math.sqrt(2.0 / math.pi) * (x + 0.044715 * (x * x * x)))))


def _softplus(x):
    return jnp.maximum(x, 0.0) + jnp.log(1.0 + jnp.exp(-jnp.abs(x)))


def _rms(x, width=None):
    width = x.shape[-1] if width is None else width
    return x * lax.rsqrt(jnp.sum(x * x, axis=-1, keepdims=True) / width + RMS_EPS)


def _split_dot(x, w_bf16):
    hi = x.astype(BF16)
    lo = (x - hi.astype(F32)).astype(BF16)
    return _dot(hi, w_bf16) + _dot(lo, w_bf16)


def _head_sum(x, e_ref, et_ref):
    return _split_dot(_split_dot(x, e_ref[...]), et_ref[...])


def _head_indicator(width, head):
    e = (jnp.arange(width)[:, None] // head == jnp.arange(LANES)[None, :]).astype(BF16)
    return e, e.T


def _tile(length, pref):
    t = min(pref, length)
    assert length % t == 0, (length, t)
    return t


def _mod_spec(d):
    return pl.BlockSpec((None, 1, d), lambda b, t: (b, 0, 0))


def _ada_kernel(cond_ref, w_ref, b_ref, o_ref):
    c = cond_ref[...]
    o_ref[...] = _dot(_silu(c), w_ref[...], HIGHEST) + b_ref[...]


def _ada_modulation(cond, ada_w, ada_b):
    depth, d, d3 = ada_w.shape
    r = cond.shape[0]
    return pl.pallas_call(
        _ada_kernel,
        out_shape=jax.ShapeDtypeStruct((depth, r, d3), F32),
        grid=(depth, d3 // d),
        in_specs=[pl.BlockSpec((r, d), lambda i, j: (0, 0)),
                  pl.BlockSpec((None, d, d), lambda i, j: (i, 0, j)),
                  pl.BlockSpec((None, 1, d), lambda i, j: (i, 0, j))],
        out_specs=pl.BlockSpec((None, r, d), lambda i, j: (i, 0, j)),
        compiler_params=_params(2),
        name="ada_modulation",
    )(cond, ada_w, ada_b.reshape(depth, 1, d3))


def _sgu_kernel(x_ref, shift_ref, scale_ref, gate_ref, w_in_ref, gain_ref, w_s_ref, bias_ref, w_out_ref,
                o_ref, v_scr, s_scr):
    t = x_ref.shape[0]
    w = gain_ref.shape[-1]
    gw = w // SGU_GROUPS
    x = x_ref[...]
    hb = (_rms(x) * (1.0 + scale_ref[...]) + shift_ref[...]).astype(BF16)
    ssq = jnp.zeros((t, 1), F32)
    for g in range(SGU_GROUPS):
        vg = _gelu(_dot(hb, w_in_ref[:, w + g * gw:w + (g + 1) * gw]))
        v_scr[:, g * gw:(g + 1) * gw] = vg
        ssq = ssq + jnp.sum(vg * vg, axis=-1, keepdims=True)
    rs = lax.rsqrt(ssq / w + RMS_EPS)
    for g in range(SGU_GROUPS):
        cols = slice(g * gw, (g + 1) * gw)
        vn = (v_scr[:, cols] * rs * gain_ref[:, cols]).astype(BF16)
        u = _gelu(_dot(hb, w_in_ref[:, cols]))
        sz = _silu(_dot(hb, w_in_ref[:, 2 * w + g * gw:2 * w + (g + 1) * gw]))
        for c in range(t // SGU_CHUNK):
            rows = slice(c * SGU_CHUNK, (c + 1) * SGU_CHUNK)
            mixed = _dot(w_s_ref[g], vn[rows]) + bias_ref[:, cols]
            s_scr[rows, cols] = ((u[rows] * mixed) * sz[rows]).astype(BF16)
    o_ref[...] = x + gate_ref[...] * _dot(s_scr[...], w_out_ref[...])


def _sgu_layer(x, shift, scale, gate, w_in, gain, w_s, b_s, w_out):
    b, l, d = x.shape
    w = gain.shape[-1]
    t = _tile(l, 512)
    assert t % SGU_CHUNK == 0 and w % (SGU_GROUPS * LANES) == 0
    bias = jnp.repeat(b_s.T, w // SGU_GROUPS, axis=1)
    tok = pl.BlockSpec((None, t, d), lambda bi, ti: (bi, ti, 0))
    return pl.pallas_call(
        _sgu_kernel,
        out_shape=jax.ShapeDtypeStruct(x.shape, F32),
        grid=(b, l // t),
        in_specs=[tok, _mod_spec(d), _mod_spec(d), _mod_spec(d),
                  _const_spec((d, 3 * w)), _const_spec((1, w)), _const_spec(w_s.shape),
                  _const_spec(bias.shape), _const_spec((w, d))],
        out_specs=tok,
        scratch_shapes=[pltpu.VMEM((t, w), F32), pltpu.VMEM((t, w), BF16)],
        compiler_params=_params(2),
        name="sgu_layer",
    )(x, shift, scale, gate, w_in.astype(BF16), gain.reshape(1, w), w_s.astype(BF16), bias, w_out.astype(BF16))


def _rwkv_feat_kernel(x_ref, xp_ref, xn_ref, shift_ref, scale_ref, mu_ref, w_in_ref, wl1_ref, wl2_ref, w0_ref,
                      al1_ref, al2_ref, a0_ref, kk_w_ref, ka_ref, rk_ref, e_ref, et_ref,
                      r_o, v_o, kk_o, z_o, bonus_o, lw_o, kd_o, bb_o):
    t = x_ref.shape[0]
    ti = pl.program_id(1)
    nt = pl.num_programs(1)
    sc = 1.0 + scale_ref[...]
    sh = shift_ref[...]
    h = _rms(x_ref[...]) * sc + sh
    hp = (_rms(xp_ref[SUBLANES - 1:SUBLANES, :]) * sc + sh) * (ti > 0).astype(F32)
    hn = (_rms(xn_ref[0:1, :]) * sc + sh) * (ti < nt - 1).astype(F32)
    row = lax.broadcasted_iota(jnp.int32, (t, 1), 0)
    prev = jnp.where(row == 0, hp, pltpu.roll(h, 1, 0))
    nxt = jnp.where(row == t - 1, hn, pltpu.roll(h, t - 1, 0))
    xx = 0.5 * (prev + nxt) - h

    def lerp(c):
        return (h + xx * mu_ref[c:c + 1, :]).astype(BF16)

    r = _dot(lerp(0), w_in_ref[0])
    k = _dot(lerp(1), w_in_ref[1])
    v = _dot(lerp(2), w_in_ref[2])
    z_o[...] = _dot(lerp(3), w_in_ref[3])
    r_o[...] = r
    v_o[...] = v
    kkr = k * kk_w_ref[...]
    kkn = kkr * _split_dot(lax.rsqrt(_split_dot(kkr * kkr, e_ref[...]) + 1e-12), et_ref[...])
    kk_o[...] = kkn
    t_w = jnp.tanh(_dot(lerp(4), wl1_ref[...])).astype(BF16)
    t_a = _dot(lerp(5), al1_ref[...]).astype(BF16)
    kd_sum = jnp.zeros_like(k)
    for n in range(2):
        log_w = -_softplus(-(w0_ref[n:n + 1, :] + _dot(t_w, wl2_ref[n]))) - 0.5
        lw_o[n] = -jnp.exp(log_w)
        a = _sigmoid(a0_ref[n:n + 1, :] + _dot(t_a, al2_ref[n]))
        kd = k * (1.0 + (a - 1.0) * ka_ref[...])
        kd_o[n] = kd
        bb_o[n] = kkn * a
        kd_sum = kd_sum + kd
    bonus_o[...] = _head_sum(r * kd_sum * rk_ref[...], e_ref, et_ref) * v


def _rwkv_features(x, shift, scale, mu, w_in, w_lora1, w_lora2, w0, a_lora1, a_lora2, a0, k_k, k_a, r_k):
    b, l, d = x.shape
    e = w_in.shape[-1]
    t = _tile(l, 256)
    nb8 = l // SUBLANES
    tok = pl.BlockSpec((None, t, d), lambda bi, ti: (bi, ti, 0))
    prev8 = pl.BlockSpec((None, SUBLANES, d), lambda bi, ti: (bi, jnp.maximum(ti * (t // SUBLANES) - 1, 0), 0))
    next8 = pl.BlockSpec((None, SUBLANES, d), lambda bi, ti: (bi, jnp.minimum((ti + 1) * (t // SUBLANES), nb8 - 1), 0))
    rank = w_lora1.shape[-1]
    assert 2 * rank == LANES and a_lora1.shape[-1] == rank
    zr = jnp.zeros((rank, e), F32)
    wl1 = jnp.concatenate([w_lora1[0], w_lora1[1]], axis=1).astype(BF16)
    al1 = jnp.concatenate([a_lora1[0], a_lora1[1]], axis=1).astype(BF16)
    wl2 = jnp.stack([jnp.concatenate([w_lora2[0], zr]), jnp.concatenate([zr, w_lora2[1]])]).astype(BF16)
    al2 = jnp.stack([jnp.concatenate([a_lora2[0], zr]), jnp.concatenate([zr, a_lora2[1]])]).astype(BF16)
    ind, ind_t = _head_indicator(e, RWKV_HEAD)
    out_tok = pl.BlockSpec((None, t, e), lambda bi, ti: (bi, ti, 0))
    out_dir = pl.BlockSpec((2, None, t, e), lambda bi, ti: (0, bi, ti, 0))
    shared = jax.ShapeDtypeStruct((b, l, e), F32)
    per_dir = jax.ShapeDtypeStruct((2, b, l, e), F32)
    return pl.pallas_call(
        _rwkv_feat_kernel,
        out_shape=(shared,) * 5 + (per_dir,) * 3,
        grid=(b, l // t),
        in_specs=[tok, prev8, next8, _mod_spec(d), _mod_spec(d),
                  _const_spec(mu.shape), _const_spec(w_in.shape), _const_spec(wl1.shape), _const_spec(wl2.shape),
                  _const_spec(w0.shape), _const_spec(al1.shape), _const_spec(al2.shape), _const_spec(a0.shape),
                  _const_spec((1, e)), _const_spec((1, e)), _const_spec((1, e)),
                  _const_spec(ind.shape), _const_spec(ind_t.shape)],
        out_specs=(out_tok,) * 5 + (out_dir,) * 3,
        compiler_params=_params(2),
        name="rwkv_features",
    )(x, x, x, shift, scale, mu, w_in.astype(BF16), wl1, wl2, w0, al1, al2, a0,
      k_k.reshape(1, e), k_a.reshape(1, e), r_k.reshape(1, e), ind, ind_t)


def _unit_triangular_inverse(a, eye, r2, c2, size):
    n = range(len(a))
    blk = (r2 >> 3) == (c2 >> 3)
    d = [jnp.where(blk, a[i], 0.0) for i in n]
    d_b = [d[i].astype(BF16) for i in n]
    d2 = [_dot(d_b[i], d_b[i]) for i in n]
    d2_b = [d2[i].astype(BF16) for i in n]
    d4 = [_dot(d2_b[i], d2_b[i]) for i in n]
    inv = [_bdot(eye - d[i], eye + d2[i]) for i in n]
    inv = [_bdot(inv[i], eye + d4[i]) for i in n]
    s = 8
    while s < size:
        sh = s.bit_length() - 1
        off = ((r2 >> (sh + 1)) == (c2 >> (sh + 1))) & ((r2 >> sh) != (c2 >> sh))
        inv_b = [inv[i].astype(BF16) for i in n]
        e_inv = [_dot(jnp.where(off, a[i], 0.0).astype(BF16), inv_b[i]).astype(BF16) for i in n]
        inv = [inv[i] - _dot(inv_b[i], e_inv[i]) for i in n]
        s *= 2
    return inv


def _rwkv_scan_kernel(lw_ref, kd_ref, bb_ref, v_ref, kk_ref, r_ref, s0_ref, y_ref, sf_ref, st_scr):
    c, hn = v_ref.shape
    half = LANES // 2
    assert c == half == RWKV_HEAD
    ci = pl.program_id(2)
    nc = pl.num_programs(2)

    @pl.when(ci == 0)
    def _():
        st_scr[...] = s0_ref[...]

    sgn = jnp.where(pl.program_id(0) == 0, 1, -1)
    rc = lax.broadcasted_iota(jnp.int32, (c, c), 0)
    cc = lax.broadcasted_iota(jnp.int32, (c, c), 1)
    cum = (sgn * (rc - cc) >= 0).astype(F32)
    lw = lw_ref[...]
    cin = _dot(cum, lw, HIGHEST)
    ctot = _dot_tn(lw, jnp.ones((c, LANES), F32), HIGHEST)
    p_in = jnp.exp(cin)
    inv_in = jnp.exp(-cin)
    p_rest = jnp.exp(jnp.sum(lw, axis=0, keepdims=True) - cin)
    rt = r_ref[...] * p_in
    kt = kk_ref[...] * jnp.exp(cin - lw)
    kbar = kd_ref[...] * inv_in
    bbar = bb_ref[...] * inv_in
    khat = kd_ref[...] * p_rest
    bhat = bb_ref[...] * p_rest
    v = v_ref[...]

    n2 = 2 * c
    r2 = lax.broadcasted_iota(jnp.int32, (n2, n2), 0)
    c2 = lax.broadcasted_iota(jnp.int32, (n2, n2), 1)
    tdiff = sgn * ((r2 & (c - 1)) - (c2 & (c - 1)))
    strict = tdiff > 0
    incl = tdiff >= 0
    eye = (r2 == c2).astype(F32)
    first = lax.broadcasted_iota(jnp.int32, (1, LANES), 1) < half

    def stack(xp):
        return jnp.concatenate([jnp.where(first, xp, 0.0), jnp.where(first, 0.0, xp)], axis=0)

    pairs = range(hn // LANES)
    sls = [slice(p * LANES, (p + 1) * LANES) for p in pairs]
    v_s = [stack(v[:, sl]).astype(BF16) for sl in sls]
    lhs = [jnp.concatenate([stack(kt[:, sl]), stack(rt[:, sl])], axis=0).astype(BF16) for sl in sls]
    gb = [_dot_nt(lhs[p], stack(bbar[:, sls[p]]).astype(BF16)) for p in pairs]
    gk = [_dot_nt(lhs[p], stack(kbar[:, sls[p]]).astype(BF16)) for p in pairs]
    states = [st_scr[p] for p in pairs]
    m1 = [_dot(lhs[p], states[p].astype(BF16)) for p in pairs]
    a_m = [jnp.where(strict, gb[p][:n2], 0.0) for p in pairs]
    qb = [jnp.where(incl, gb[p][n2:], 0.0) for p in pairs]
    b_m = [jnp.where(strict, gk[p][:n2], 0.0).astype(BF16) for p in pairs]
    qk = [jnp.where(incl, gk[p][n2:], 0.0) for p in pairs]
    rhs = [m1[p][:n2] + _dot(b_m[p], v_s[p]) for p in pairs]
    t_inv = _unit_triangular_inverse(a_m, eye, r2, c2, c)
    u = [_bdot(t_inv[p], rhs[p]) for p in pairs]
    vu = [jnp.concatenate([v_s[p], u[p].astype(BF16)], axis=0) for p in pairs]
    for p in pairs:
        y = m1[p][n2:] + _dot(jnp.concatenate([qk[p], -qb[p]], axis=1).astype(BF16), vu[p])
        y_ref[:, sls[p]] = y[:c] + y[c:]
    for p in pairs:
        x_hat = jnp.concatenate([stack(khat[:, sls[p]]), -stack(bhat[:, sls[p]])], axis=0).astype(BF16)
        st_scr[p] = jnp.exp(ctot[sls[p]]) * states[p] + _dot_tn(x_hat, vu[p])

    @pl.when(ci == nc - 1)
    def _():
        sf_ref[...] = st_scr[...]


def _rwkv_scan(lw, kd, bb, v, kk, r, s0):
    _, b, l, hn = lw.shape
    c = RWKV_CHUNK
    nc = l // c
    npair = hn // LANES
    assert l % c == 0 and hn % LANES == 0

    def chunk(d, ci):
        return ci + d * (nc - 1 - 2 * ci)

    per_dir = pl.BlockSpec((None, None, c, hn), lambda d, bi, ci: (d, bi, chunk(d, ci), 0))
    shared = pl.BlockSpec((None, c, hn), lambda d, bi, ci: (bi, chunk(d, ci), 0))
    st = pl.BlockSpec((None, None, npair, LANES, LANES), lambda d, bi, ci: (d, bi, 0, 0, 0))
    return pl.pallas_call(
        _rwkv_scan_kernel,
        out_shape=(jax.ShapeDtypeStruct((2, b, l, hn), F32), jax.ShapeDtypeStruct(s0.shape, F32)),
        grid=(2, b, nc),
        in_specs=[per_dir, per_dir, per_dir, shared, shared, shared, st],
        out_specs=(per_dir, st),
        scratch_shapes=[pltpu.VMEM((npair, LANES, LANES), F32)],
        compiler_params=_params(3),
        name="rwkv_scan",
    )(lw, kd, bb, v, kk, r, s0)


def _rwkv_out_kernel(x_ref, y_ref, bonus_ref, z_ref, gate_ref, lng_ref, lnb_ref, e_ref, et_ref, w_out_ref, o_ref):
    y = y_ref[0] + y_ref[1]
    mean = _head_sum(y, e_ref, et_ref) * (1.0 / RWKV_HEAD)
    yc = y - mean
    var = _head_sum(yc * yc, e_ref, et_ref) * (1.0 / RWKV_HEAD)
    yn = yc * lax.rsqrt(var + RWKV_LN_EPS) * lng_ref[...] + lnb_ref[...]
    out = (yn + bonus_ref[...]) * _silu(z_ref[...])
    o_ref[...] = x_ref[...] + gate_ref[...] * _dot(out.astype(BF16), w_out_ref[...])


def _rwkv_output(x, y, bonus, z, gate, ln_gain, ln_bias, w_out):
    b, l, d = x.shape
    e = y.shape[-1]
    t = _tile(l, 512)
    ind, ind_t = _head_indicator(e, RWKV_HEAD)
    tok_d = pl.BlockSpec((None, t, d), lambda bi, ti: (bi, ti, 0))
    tok_e = pl.BlockSpec((None, t, e), lambda bi, ti: (bi, ti, 0))
    return pl.pallas_call(
        _rwkv_out_kernel,
        out_shape=jax.ShapeDtypeStruct(x.shape, F32),
        grid=(b, l // t),
        in_specs=[tok_d, pl.BlockSpec((2, None, t, e), lambda bi, ti: (0, bi, ti, 0)), tok_e, tok_e, _mod_spec(d),
                  _const_spec((1, e)), _const_spec((1, e)), _const_spec(ind.shape), _const_spec(ind_t.shape),
                  _const_spec(w_out.shape)],
        out_specs=tok_d,
        compiler_params=_params(2),
        name="rwkv_output",
    )(x, y, bonus, z, gate, ln_gain.reshape(1, e), ln_bias.reshape(1, e), ind, ind_t, w_out.astype(BF16))


def _mla_proj_kernel(*refs, need_q, rotate):
    if need_q:
        (x_ref, shift_ref, scale_ref, w_in_ref, qn_ref, kvn_ref, w_uq_ref, w_uk_ref, w_uvt_ref, gq_ref, gk_ref,
         ta_ref, tb_ref, q_o, k_o, vt_o, z_o) = refs
    else:
        (x_ref, shift_ref, scale_ref, w_in_ref, kvn_ref, w_uk_ref, w_uvt_ref, gk_ref, ta_ref, tb_ref, k_o, vt_o) = refs
    hb = (_rms(x_ref[...]) * (1.0 + scale_ref[...]) + shift_ref[...]).astype(BF16)
    o_kv = MLA_Q_RANK if need_q else 0
    o_kr = o_kv + MLA_KV_RANK
    first = lax.broadcasted_iota(jnp.int32, (1, LANES), 1) < MLA_ROPE
    ta = ta_ref[...]
    tb = tb_ref[...]

    def rope_tile(t2, gain_ref, use_rot, use_plain):
        sw = pltpu.roll(t2, MLA_ROPE, 1)
        g_a = gain_ref[1:2, :]
        g_b = gain_ref[2:3, :]
        m_b = tb * g_b
        if not use_rot:
            return sw * jnp.where(first, 0.0, m_b)
        if not use_plain:
            m_b = jnp.where(first, m_b, 0.0)
        return t2 * (ta * g_a) + sw * m_b

    ckv = (_rms(_dot(hb, w_in_ref[:, o_kv:o_kr])) * kvn_ref[...]).astype(BF16)
    vt = _dot_nt(w_uvt_ref[...], ckv)
    kr = _dot(hb, w_in_ref[:, o_kr:o_kr + LANES])
    kr_ssq = jnp.sum(jnp.where(first, kr * kr, 0.0), axis=-1, keepdims=True)
    kr_out = rope_tile(kr, gk_ref, rotate, not rotate)
    ones_rows = (lax.broadcasted_iota(jnp.int32, (MLA_VT_ROWS - MLA_VDIM, vt.shape[1]), 0) == 0).astype(vt_o.dtype)
    for h in range(MLA_HEADS):
        vt_o[h, :MLA_VDIM, :] = vt[h * MLA_VDIM:(h + 1) * MLA_VDIM].astype(vt_o.dtype)
        vt_o[h, MLA_VDIM:, :] = ones_rows
    for h in range(MLA_HEADS):
        kn = _dot(ckv, w_uk_ref[:, h * LANES:(h + 1) * LANES])
        rs = lax.rsqrt((jnp.sum(kn * kn, axis=-1, keepdims=True) + kr_ssq) / MLA_QK_DIM + RMS_EPS)
        k_o[h] = jnp.concatenate([kn * rs * gk_ref[0:1, :], kr_out * rs], axis=1).astype(k_o.dtype)
    if need_q:
        z_o[...] = _dot(hb, w_in_ref[:, o_kr + LANES:]).astype(z_o.dtype)
        cq = (_rms(_dot(hb, w_in_ref[:, :MLA_Q_RANK])) * qn_ref[...]).astype(BF16)
        for h in range(MLA_HEADS):
            t1 = _dot(cq, w_uq_ref[:, h * 2 * LANES:h * 2 * LANES + LANES])
            t2 = _dot(cq, w_uq_ref[:, h * 2 * LANES + LANES:(h + 1) * 2 * LANES])
            ssq = jnp.sum(t1 * t1, axis=-1, keepdims=True) + jnp.sum(jnp.where(first, t2 * t2, 0.0), axis=-1, keepdims=True)
            rs = lax.rsqrt(ssq / MLA_QK_DIM + RMS_EPS) * (MLA_SCALE * math.log2(math.e))
            q_o[h] = jnp.concatenate([t1 * rs * gq_ref[0:1, :], rope_tile(t2, gq_ref, True, True) * rs],
                                     axis=1).astype(q_o.dtype)


def _swap_halves(w):
    half = w.shape[-1] // 2
    return jnp.concatenate([w[..., half:], w[..., :half]], axis=-1)


def _qk_gain_rows(g):
    gn, gr = g[:MLA_NOPE], g[MLA_NOPE:]
    zero = jnp.zeros_like(gr)
    rows = jnp.stack([gn, jnp.concatenate([gr, zero]), jnp.concatenate([_swap_halves(gr), gr])])
    return jnp.concatenate([rows, jnp.zeros((SUBLANES - 3, LANES), F32)])


def _rope_tables(length):
    rows = length // GRID_W
    row = jnp.repeat(jnp.arange(rows, dtype=F32), GRID_W)
    col = jnp.tile(jnp.arange(GRID_W, dtype=F32), rows)
    nf = MLA_ROPE // 4
    inv_freq = ROPE_BASE ** (-jnp.arange(nf, dtype=F32) / nf)
    ang = jnp.concatenate([row[:, None] * inv_freq, col[:, None] * inv_freq], axis=-1)
    cos, sin = jnp.cos(ang), jnp.sin(ang)
    ta = jnp.concatenate([cos, cos, jnp.zeros((length, MLA_ROPE), F32)], axis=-1)
    tb = jnp.concatenate([-sin, sin, jnp.ones((length, MLA_ROPE), F32)], axis=-1)
    return ta, tb


def _mla_project(x, shift, scale, w_in, q_norm, kv_norm, w_uq, w_ukv, gq, gk, ta, tb, need_q, rotate):
    b, l, d = x.shape
    hh = MLA_HEADS
    t = _tile(l, MLA_KV_TILE)
    o1, o2, o3 = MLA_Q_RANK, MLA_Q_RANK + MLA_KV_RANK, MLA_Q_RANK + MLA_KV_RANK + MLA_ROPE
    w_kr = w_in[:, o2:o3]
    cols = ([w_in[:, :o1]] if need_q else []) + [w_in[:, o1:o2], w_kr, _swap_halves(w_kr)] + ([w_in[:, o3:]] if need_q else [])
    w_in_p = jnp.concatenate(cols, axis=1).astype(BF16)
    tok = pl.BlockSpec((None, t, d), lambda bi, ti: (bi, ti, 0))
    tab = pl.BlockSpec((t, LANES), lambda bi, ti: (ti, 0))
    head_qk = pl.BlockSpec((None, hh, t, 2 * LANES), lambda bi, ti: (bi, 0, ti, 0))
    head_vt = pl.BlockSpec((None, hh, None, MLA_VT_ROWS, t), lambda bi, ti: (bi, 0, ti, 0, 0))
    qk_shape = jax.ShapeDtypeStruct((b, hh, l, 2 * LANES), BF16)
    vt_shape = jax.ShapeDtypeStruct((b, hh, l // t, MLA_VT_ROWS, t), BF16)
    kvn = kv_norm.reshape(1, -1)
    wkv = w_ukv.reshape(MLA_KV_RANK, hh, MLA_NOPE + MLA_VDIM)
    w_uk = wkv[..., :MLA_NOPE].reshape(MLA_KV_RANK, hh * MLA_NOPE).astype(BF16)
    w_uvt = wkv[..., MLA_NOPE:].reshape(MLA_KV_RANK, hh * MLA_VDIM).T.astype(BF16)
    gk_rows = _qk_gain_rows(gk)
    if need_q:
        wq = w_uq.reshape(MLA_Q_RANK, hh, MLA_QK_DIM)
        w_uq_p = jnp.concatenate([wq, _swap_halves(wq[..., MLA_NOPE:])], axis=-1).reshape(MLA_Q_RANK, hh * 2 * LANES)
        width = w_in.shape[1] - o3
        ins = (x, shift, scale, w_in_p, q_norm.reshape(1, -1), kvn, w_uq_p.astype(BF16), w_uk, w_uvt,
               _qk_gain_rows(gq), gk_rows, ta, tb)
        in_specs = [tok, _mod_spec(d), _mod_spec(d), _const_spec(w_in_p.shape), _const_spec((1, MLA_Q_RANK)),
                    _const_spec(kvn.shape), _const_spec(w_uq_p.shape), _const_spec(w_uk.shape), _const_spec(w_uvt.shape),
                    _const_spec((SUBLANES, LANES)), _const_spec((SUBLANES, LANES)), tab, tab]
        out_shape = (qk_shape, qk_shape, vt_shape, jax.ShapeDtypeStruct((b, l, width), BF16))
        out_specs = (head_qk, head_qk, head_vt, pl.BlockSpec((None, t, width), lambda bi, ti: (bi, ti, 0)))
    else:
        ins = (x, shift, scale, w_in_p, kvn, w_uk, w_uvt, gk_rows, ta, tb)
        in_specs = [tok, _mod_spec(d), _mod_spec(d), _const_spec(w_in_p.shape), _const_spec(kvn.shape),
                    _const_spec(w_uk.shape), _const_spec(w_uvt.shape), _const_spec((SUBLANES, LANES)), tab, tab]
        out_shape = (qk_shape, vt_shape)
        out_specs = (head_qk, head_vt)
    return pl.pallas_call(
        functools.partial(_mla_proj_kernel, need_q=need_q, rotate=rotate),
        out_shape=out_shape,
        grid=(b, l // t),
        in_specs=in_specs,
        out_specs=out_specs,
        compiler_params=_params(2),
        name="mla_project_q" if need_q else "mla_project_kv",
    )(*ins)


def _flash_kernel(q_ref, kc_ref, vc_ref, kl_ref, vl_ref, o_ref,
                  m_scr, acc_scr, sa_scr, sb_scr, ma_scr, mb_scr, pa_scr, pb_scr):
    n_tiles, _, tt = vl_ref.shape
    tk = sa_scr.shape[0]
    per_blk = tk // tt
    nkv = n_tiles // per_blk
    dv = o_ref.shape[-1]
    q = q_ref[...]

    def put_scores(j, s_scr, bm_scr):
        s = _dot_nt(kl_ref[pl.ds(pl.multiple_of(j * tk, tk), tk), :], q)
        s_scr[...] = s
        bm_scr[...] = jnp.max(s, axis=0, keepdims=True)

    def accumulate(s_scr, bm_scr, p_scr, j):
        m_prev = m_scr[...]
        m_new = jnp.maximum(m_prev, bm_scr[...])
        p_scr[...] = jnp.exp2(s_scr[...] - m_new).astype(BF16)
        pv = _dot(vl_ref[j * per_blk], p_scr[0:tt, :])
        for i in range(1, per_blk):
            pv = pv + _dot(vl_ref[j * per_blk + i], p_scr[i * tt:(i + 1) * tt, :])
        acc_scr[...] = jnp.exp2(m_prev - m_new) * acc_scr[...] + pv
        m_scr[...] = m_new

    put_scores(0, sa_scr, ma_scr)
    st = _dot_nt(kc_ref[...], q)
    m0 = jnp.max(st, axis=0, keepdims=True)
    m_scr[...] = m0
    acc_scr[...] = _dot(vc_ref[0], jnp.exp2(st - m0).astype(BF16))

    def body(i, carry):
        j = 2 * i
        put_scores(j + 1, sb_scr, mb_scr)
        accumulate(sa_scr, ma_scr, pa_scr, j)
        put_scores(jnp.minimum(j + 2, nkv - 1), sa_scr, ma_scr)
        accumulate(sb_scr, mb_scr, pb_scr, j + 1)
        return carry

    if nkv // 2:
        lax.fori_loop(0, nkv // 2, body, 0)
    if nkv % 2:
        accumulate(sa_scr, ma_scr, pa_scr, nkv - 1)
    o_ref[...] = jnp.transpose(acc_scr[:dv, :] / acc_scr[dv:dv + 1, :]).astype(o_ref.dtype)


def _mla_attention(q, k_ctx, vt_ctx, k_lat, vt_lat):
    b, hh, l, dq = q.shape
    lc = k_ctx.shape[2]
    _, _, nkv, dva, tt = vt_lat.shape
    dv = MLA_VDIM
    tq = _tile(l, 2048)
    tk = tt * (2 if nkv % 2 == 0 else 1)
    return pl.pallas_call(
        _flash_kernel,
        out_shape=jax.ShapeDtypeStruct((b, l, hh * dv), BF16),
        grid=(b, hh, l // tq),
        in_specs=[pl.BlockSpec((None, None, tq, dq), lambda bi, h, qi: (bi, h, qi, 0)),
                  pl.BlockSpec((None, None, lc, dq), lambda bi, h, qi: (bi, h, 0, 0)),
                  pl.BlockSpec((None, None, 1, dva, lc), lambda bi, h, qi: (bi, h, 0, 0, 0)),
                  pl.BlockSpec((None, None, l, dq), lambda bi, h, qi: (bi, h, 0, 0)),
                  pl.BlockSpec((None, None, nkv, dva, tt), lambda bi, h, qi: (bi, h, 0, 0, 0))],
        out_specs=pl.BlockSpec((None, tq, dv), lambda bi, h, qi: (bi, qi, h)),
        scratch_shapes=[pltpu.VMEM((1, tq), F32), pltpu.VMEM((dva, tq), F32),
                        pltpu.VMEM((tk, tq), F32), pltpu.VMEM((tk, tq), F32),
                        pltpu.VMEM((1, tq), F32), pltpu.VMEM((1, tq), F32),
                        pltpu.VMEM((tk, tq), BF16), pltpu.VMEM((tk, tq), BF16)],
        compiler_params=_params(3),
        name="mla_attention",
    )(q, k_ctx, vt_ctx, k_lat, vt_lat)


def _gated_out_kernel(x_ref, a_ref, z_ref, gate_ref, w_ref, o_ref):
    s = (a_ref[...].astype(F32) * _silu(z_ref[...].astype(F32))).astype(BF16)
    o_ref[...] = x_ref[...] + gate_ref[...] * _dot(s, w_ref[...])


def _gated_out(x, a, z, gate, w_out):
    b, l, d = x.shape
    e = a.shape[-1]
    t = _tile(l, 512)
    tok_d = pl.BlockSpec((None, t, d), lambda bi, ti: (bi, ti, 0))
    tok_e = pl.BlockSpec((None, t, e), lambda bi, ti: (bi, ti, 0))
    return pl.pallas_call(
        _gated_out_kernel,
        out_shape=jax.ShapeDtypeStruct(x.shape, F32),
        grid=(b, l // t),
        in_specs=[tok_d, tok_e, tok_e, _mod_spec(d), _const_spec(w_out.shape)],
        out_specs=tok_d,
        compiler_params=_params(2),
        name="mla_output",
    )(x, a, z, gate, w_out.astype(BF16))


def kernel(x, c, ctx, c_ctx, ada_w, ada_b, sgu_w_in, sgu_gain, sgu_w_s, sgu_b_s, sgu_w_out, rwkv_mu, rwkv_w_in, rwkv_w_lora1, rwkv_w_lora2, rwkv_w0, rwkv_a_lora1, rwkv_a_lora2, rwkv_a0, rwkv_k_k, rwkv_k_a, rwkv_r_k, rwkv_ln_gain, rwkv_ln_bias, rwkv_w_out, mla_w_in, mla_q_norm, mla_kv_norm, mla_w_uq, mla_w_ukv, mla_qk_gain_q, mla_qk_gain_k, mla_w_out):
    bsz, length, d = x.shape
    depth = ada_w.shape[0]
    n_cond = -(-(bsz + 1) // SUBLANES) * SUBLANES
    cond = jnp.concatenate([c, c_ctx[None], jnp.zeros((n_cond - bsz - 1, d), F32)])
    mods = _ada_modulation(cond, ada_w, ada_b)
    ctx_readers = [i for i in range(depth) if i % N_MIXERS != 0]
    last_ctx_reader = ctx_readers[-1] if ctx_readers else -1
    ta, tb = _rope_tables(length)
    for i in range(depth):
        kind, j = i % N_MIXERS, i // N_MIXERS
        update_ctx = i < last_ctx_reader
        shift, scale, gate = (mods[i, :bsz, s * d:(s + 1) * d][:, None, :] for s in range(3))
        c_shift, c_scale, c_gate = (jnp.broadcast_to(mods[i, bsz, s * d:(s + 1) * d], (bsz, 1, d)) for s in range(3))
        if kind == 0:
            sgu_args = (sgu_w_in[j], sgu_gain[j], sgu_w_s[j], sgu_b_s[j], sgu_w_out[j])
            x = _sgu_layer(x, shift, scale, gate, *sgu_args)
            if update_ctx:
                ctx = _sgu_layer(ctx, c_shift, c_scale, c_gate, *sgu_args)
        elif kind == 1:
            feat_args = (rwkv_mu[j], rwkv_w_in[j], rwkv_w_lora1[j], rwkv_w_lora2[j], rwkv_w0[j],
                         rwkv_a_lora1[j], rwkv_a_lora2[j], rwkv_a0[j], rwkv_k_k[j], rwkv_k_a[j],
                         rwkv_r_k[j].reshape(-1))
            out_args = (rwkv_ln_gain[j], rwkv_ln_bias[j], rwkv_w_out[j])
            r_c, v_c, kk_c, z_c, bonus_c, lw_c, kd_c, bb_c = _rwkv_features(ctx, c_shift, c_scale, *feat_args)
            state0 = jnp.zeros((2, bsz, r_c.shape[-1] // LANES, LANES, LANES), F32)
            y_c, s_ctx = _rwkv_scan(lw_c, kd_c, bb_c, v_c, kk_c, r_c, state0)
            r_l, v_l, kk_l, z_l, bonus_l, lw_l, kd_l, bb_l = _rwkv_features(x, shift, scale, *feat_args)
            y_l, _ = _rwkv_scan(lw_l, kd_l, bb_l, v_l, kk_l, r_l, s_ctx)
            x = _rwkv_output(x, y_l, bonus_l, z_l, gate, *out_args)
            if update_ctx:
                ctx = _rwkv_output(ctx, y_c, bonus_c, z_c, c_gate, *out_args)
        else:
            assert not update_ctx, "context-stream update after an MLA layer is not implemented"
            mla_args = (mla_w_in[j], mla_q_norm[j], mla_kv_norm[j], mla_w_uq[j], mla_w_ukv[j],
                        mla_qk_gain_q[j], mla_qk_gain_k[j])
            n_ctx = ctx.shape[1]
            k_c, v_c = _mla_project(ctx, c_shift, c_scale, *mla_args, ta[:n_ctx], tb[:n_ctx], need_q=False, rotate=False)
            q_l, k_l, v_l, z_l = _mla_project(x, shift, scale, *mla_args, ta, tb, need_q=True, rotate=True)
            o_l = _mla_attention(q_l, k_c, v_c, k_l, v_l)
            x = _gated_out(x, o_l, z_l, gate, mla_w_out[j])
    return x
```

```python
import functools
import math

import jax
import jax.numpy as jnp
from jax import lax
from jax.experimental import pallas as pl
from jax.experimental.pallas import tpu as pltpu

F32 = jnp.float32
BF16 = jnp.bfloat16
HIGHEST = lax.Precision.HIGHEST

N_MIXERS = 3
RMS_EPS = 1e-6
SGU_CHUNK = 128
SGU_GROUPS = 8
RWKV_HEAD = 64
RWKV_LN_EPS = 64e-5
RWKV_CHUNK = 64
MLA_HEADS = 16
MLA_NOPE = 128
MLA_ROPE = 64
MLA_VDIM = 128
MLA_Q_RANK = 768
MLA_KV_RANK = 256
MLA_QK_DIM = MLA_NOPE + MLA_ROPE
MLA_SCALE = MLA_QK_DIM ** -0.5
MLA_KV_TILE = 512
MLA_VT_ROWS = MLA_VDIM + 16
GRID_W = 64
ROPE_BASE = 10000.0

LANES = 128
SUBLANES = 8
VMEM_LIMIT = 56 * 1024 * 1024


def _params(n_axes):
    return pltpu.CompilerParams(dimension_semantics=("arbitrary",) * n_axes, vmem_limit_bytes=VMEM_LIMIT)


def _const_spec(shape):
    nd = len(shape)
    return pl.BlockSpec(shape, lambda *_: (0,) * nd, pipeline_mode=pl.Buffered(1))


def _dot(a, b, precision=None):
    return jnp.dot(a, b, preferred_element_type=F32, precision=precision)


def _dot_nt(a, b, precision=None):
    return lax.dot_general(a, b, (((1,), (1,)), ((), ())), preferred_element_type=F32, precision=precision)


def _dot_tn(a, b, precision=None):
    return lax.dot_general(a, b, (((0,), (0,)), ((), ())), preferred_element_type=F32, precision=precision)


def _bdot(a, b):
    return _dot(a.astype(BF16), b.astype(BF16))


def _sigmoid(x):
    return 1.0 / (1.0 + jnp.exp(-x))


def _silu(x):
    return x * _sigmoid(x)


def _gelu(x):
    return x * (0.5 * (1.0 + jnp.tanh(math.sqrt(2.0 / math.pi) * (x + 0.044715 * (x * x * x)))))


def _softplus(x):
    return jnp.maximum(x, 0.0) + jnp.log(1.0 + jnp.exp(-jnp.abs(x)))


def _rms(x, width=None):
    width = x.shape[-1] if width is None else width
    return x * lax.rsqrt(jnp.sum(x * x, axis=-1, keepdims=True) / width + RMS_EPS)


def _split_dot(x, w_bf16):
    hi = x.astype(BF16)
    lo = (x - hi.astype(F32)).astype(BF16)
    return _dot(hi, w_bf16) + _dot(lo, w_bf16)


def _head_sum(x, e_ref, et_ref):
    return _split_dot(_split_dot(x, e_ref[...]), et_ref[...])


def _head_indicator(width, head):
    e = (jnp.arange(width)[:, None] // head == jnp.arange(LANES)[None, :]).astype(BF16)
    return e, e.T


def _tile(length, pref):
    t = min(pref, length)
    assert length % t == 0, (length, t)
    return t


def _mod_spec(d):
    return pl.BlockSpec((None, 1, d), lambda b, t: (b, 0, 0))


def _ada_kernel(cond_ref, w_ref, b_ref, o_ref):
    c = cond_ref[...]
    o_ref[...] = _dot(_silu(c), w_ref[...], HIGHEST) + b_ref[...]


def _ada_modulation(cond, ada_w, ada_b):
    depth, d, d3 = ada_w.shape
    r = cond.shape[0]
    return pl.pallas_call(
        _ada_kernel,
        out_shape=jax.ShapeDtypeStruct((depth, r, d3), F32),
        grid=(depth, d3 // d),
        in_specs=[pl.BlockSpec((r, d), lambda i, j: (0, 0)),
                  pl.BlockSpec((None, d, d), lambda i, j: (i, 0, j)),
                  pl.BlockSpec((None, 1, d), lambda i, j: (i, 0, j))],
        out_specs=pl.BlockSpec((None, r, d), lambda i, j: (i, 0, j)),
        compiler_params=_params(2),
        name="ada_modulation",
    )(cond, ada_w, ada_b.reshape(depth, 1, d3))


def _sgu_kernel(x_ref, shift_ref, scale_ref, gate_ref, w_in_ref, gain_ref, w_s_ref, bias_ref, w_out_ref,
                o_ref, v_scr, s_scr):
    t = x_ref.shape[0]
    w = gain_ref.shape[-1]
    gw = w // SGU_GROUPS
    x = x_ref[...]
    hb = (_rms(x) * (1.0 + scale_ref[...]) + shift_ref[...]).astype(BF16)
    ssq = jnp.zeros((t, 1), F32)
    for g in range(SGU_GROUPS):
        vg = _gelu(_dot(hb, w_in_ref[:, w + g * gw:w + (g + 1) * gw]))
        v_scr[:, g * gw:(g + 1) * gw] = vg
        ssq = ssq + jnp.sum(vg * vg, axis=-1, keepdims=True)
    rs = lax.rsqrt(ssq / w + RMS_EPS)
    for g in range(SGU_GROUPS):
        cols = slice(g * gw, (g + 1) * gw)
        vn = (v_scr[:, cols] * rs * gain_ref[:, cols]).astype(BF16)
        u = _gelu(_dot(hb, w_in_ref[:, cols]))
        sz = _silu(_dot(hb, w_in_ref[:, 2 * w + g * gw:2 * w + (g + 1) * gw]))
        for c in range(t // SGU_CHUNK):
            rows = slice(c * SGU_CHUNK, (c + 1) * SGU_CHUNK)
            mixed = _dot(w_s_ref[g], vn[rows]) + bias_ref[:, cols]
            s_scr[rows, cols] = ((u[rows] * mixed) * sz[rows]).astype(BF16)
    o_ref[...] = x + gate_ref[...] * _dot(s_scr[...], w_out_ref[...])


def _sgu_layer(x, shift, scale, gate, w_in, gain, w_s, b_s, w_out):
    b, l, d = x.shape
    w = gain.shape[-1]
    t = _tile(l, 512)
    assert t % SGU_CHUNK == 0 and w % (SGU_GROUPS * LANES) == 0
    bias = jnp.repeat(b_s.T, w // SGU_GROUPS, axis=1)
    tok = pl.BlockSpec((None, t, d), lambda bi, ti: (bi, ti, 0))
    return pl.pallas_call(
        _sgu_kernel,
        out_shape=jax.ShapeDtypeStruct(x.shape, F32),
        grid=(b, l // t),
        in_specs=[tok, _mod_spec(d), _mod_spec(d), _mod_spec(d),
                  _const_spec((d, 3 * w)), _const_spec((1, w)), _const_spec(w_s.shape),
                  _const_spec(bias.shape), _const_spec((w, d))],
        out_specs=tok,
        scratch_shapes=[pltpu.VMEM((t, w), F32), pltpu.VMEM((t, w), BF16)],
        compiler_params=_params(2),
        name="sgu_layer",
    )(x, shift, scale, gate, w_in.astype(BF16), gain.reshape(1, w), w_s.astype(BF16), bias, w_out.astype(BF16))


def _rwkv_feat_kernel(x_ref, xp_ref, xn_ref, shift_ref, scale_ref, mu_ref, w_in_ref, wl1_ref, wl2_ref, w0_ref,
                      al1_ref, al2_ref, a0_ref, kk_w_ref, ka_ref, rk_ref, e_ref, et_ref,
                      r_o, v_o, kk_o, z_o, bonus_o, lw_o, kd_o, bb_o):
    t = x_ref.shape[0]
    ti = pl.program_id(1)
    nt = pl.num_programs(1)
    sc = 1.0 + scale_ref[...]
    sh = shift_ref[...]
    h = _rms(x_ref[...]) * sc + sh
    hp = (_rms(xp_ref[SUBLANES - 1:SUBLANES, :]) * sc + sh) * (ti > 0).astype(F32)
    hn = (_rms(xn_ref[0:1, :]) * sc + sh) * (ti < nt - 1).astype(F32)
    row = lax.broadcasted_iota(jnp.int32, (t, 1), 0)
    prev = jnp.where(row == 0, hp, pltpu.roll(h, 1, 0))
    nxt = jnp.where(row == t - 1, hn, pltpu.roll(h, t - 1, 0))
    xx = 0.5 * (prev + nxt) - h

    def lerp(c):
        return (h + xx * mu_ref[c:c + 1, :]).astype(BF16)

    r = _dot(lerp(0), w_in_ref[0])
    k = _dot(lerp(1), w_in_ref[1])
    v = _dot(lerp(2), w_in_ref[2])
    z_o[...] = _dot(lerp(3), w_in_ref[3])
    r_o[...] = r
    v_o[...] = v
    kkr = k * kk_w_ref[...]
    kkn = kkr * _split_dot(lax.rsqrt(_split_dot(kkr * kkr, e_ref[...]) + 1e-12), et_ref[...])
    kk_o[...] = kkn
    t_w = jnp.tanh(_dot(lerp(4), wl1_ref[...])).astype(BF16)
    t_a = _dot(lerp(5), al1_ref[...]).astype(BF16)
    kd_sum = jnp.zeros_like(k)
    for n in range(2):
        log_w = -_softplus(-(w0_ref[n:n + 1, :] + _dot(t_w, wl2_ref[n]))) - 0.5
        lw_o[n] = -jnp.exp(log_w)
        a = _sigmoid(a0_ref[n:n + 1, :] + _dot(t_a, al2_ref[n]))
        kd = k * (1.0 + (a - 1.0) * ka_ref[...])
        kd_o[n] = kd
        bb_o[n] = kkn * a
        kd_sum = kd_sum + kd
    bonus_o[...] = _head_sum(r * kd_sum * rk_ref[...], e_ref, et_ref) * v


def _rwkv_features(x, shift, scale, mu, w_in, w_lora1, w_lora2, w0, a_lora1, a_lora2, a0, k_k, k_a, r_k):
    b, l, d = x.shape
    e = w_in.shape[-1]
    t = _tile(l, 256)
    nb8 = l // SUBLANES
    tok = pl.BlockSpec((None, t, d), lambda bi, ti: (bi, ti, 0))
    prev8 = pl.BlockSpec((None, SUBLANES, d), lambda bi, ti: (bi, jnp.maximum(ti * (t // SUBLANES) - 1, 0), 0))
    next8 = pl.BlockSpec((None, SUBLANES, d), lambda bi, ti: (bi, jnp.minimum((ti + 1) * (t // SUBLANES), nb8 - 1), 0))
    rank = w_lora1.shape[-1]
    assert 2 * rank == LANES and a_lora1.shape[-1] == rank
    zr = jnp.zeros((rank, e), F32)
    wl1 = jnp.concatenate([w_lora1[0], w_lora1[1]], axis=1).astype(BF16)
    al1 = jnp.concatenate([a_lora1[0], a_lora1[1]], axis=1).astype(BF16)
    wl2 = jnp.stack([jnp.concatenate([w_lora2[0], zr]), jnp.concatenate([zr, w_lora2[1]])]).astype(BF16)
    al2 = jnp.stack([jnp.concatenate([a_lora2[0], zr]), jnp.concatenate([zr, a_lora2[1]])]).astype(BF16)
    ind, ind_t = _head_indicator(e, RWKV_HEAD)
    out_tok = pl.BlockSpec((None, t, e), lambda bi, ti: (bi, ti, 0))
    out_dir = pl.BlockSpec((2, None, t, e), lambda bi, ti: (0, bi, ti, 0))
    shared = jax.ShapeDtypeStruct((b, l, e), F32)
    per_dir = jax.ShapeDtypeStruct((2, b, l, e), F32)
    return pl.pallas_call(
        _rwkv_feat_kernel,
        out_shape=(shared,) * 5 + (per_dir,) * 3,
        grid=(b, l // t),
        in_specs=[tok, prev8, next8, _mod_spec(d), _mod_spec(d),
                  _const_spec(mu.shape), _const_spec(w_in.shape), _const_spec(wl1.shape), _const_spec(wl2.shape),
                  _const_spec(w0.shape), _const_spec(al1.shape), _const_spec(al2.shape), _const_spec(a0.shape),
                  _const_spec((1, e)), _const_spec((1, e)), _const_spec((1, e)),
                  _const_spec(ind.shape), _const_spec(ind_t.shape)],
        out_specs=(out_tok,) * 5 + (out_dir,) * 3,
        compiler_params=_params(2),
        name="rwkv_features",
    )(x, x, x, shift, scale, mu, w_in.astype(BF16), wl1, wl2, w0, al1, al2, a0,
      k_k.reshape(1, e), k_a.reshape(1, e), r_k.reshape(1, e), ind, ind_t)


def _unit_triangular_inverse(a, eye, r2, c2, size):
    n = range(len(a))
    blk = (r2 >> 3) == (c2 >> 3)
    d = [jnp.where(blk, a[i], 0.0) for i in n]
    d_b = [d[i].astype(BF16) for i in n]
    d2 = [_dot(d_b[i], d_b[i]) for i in n]
    d2_b = [d2[i].astype(BF16) for i in n]
    d4 = [_dot(d2_b[i], d2_b[i]) for i in n]
    inv = [_bdot(eye - d[i], eye + d2[i]) for i in n]
    inv = [_bdot(inv[i], eye + d4[i]) for i in n]
    s = 8
    while s < size:
        sh = s.bit_length() - 1
        off = ((r2 >> (sh + 1)) == (c2 >> (sh + 1))) & ((r2 >> sh) != (c2 >> sh))
        inv_b = [inv[i].astype(BF16) for i in n]
        e_inv = [_dot(jnp.where(off, a[i], 0.0).astype(BF16), inv_b[i]).astype(BF16) for i in n]
        inv = [inv[i] - _dot(inv_b[i], e_inv[i]) for i in n]
        s *= 2
    return inv


def _rwkv_scan_kernel(lw_ref, kd_ref, bb_ref, v_ref, kk_ref, r_ref, s0_ref, y_ref, sf_ref, st_scr):
    c, hn = v_ref.shape
    half = LANES // 2
    assert c == half == RWKV_HEAD
    ci = pl.program_id(2)
    nc = pl.num_programs(2)

    @pl.when(ci == 0)
    def _():
        st_scr[...] = s0_ref[...]

    sgn = jnp.where(pl.program_id(0) == 0, 1, -1)
    rc = lax.broadcasted_iota(jnp.int32, (c, c), 0)
    cc = lax.broadcasted_iota(jnp.int32, (c, c), 1)
    cum = (sgn * (rc - cc) >= 0).astype(F32)
    lw = lw_ref[...]
    cin = _dot(cum, lw, HIGHEST)
    ctot = _dot_tn(lw, jnp.ones((c, LANES), F32), HIGHEST)
    p_in = jnp.exp(cin)
    inv_in = jnp.exp(-cin)
    p_rest = jnp.exp(jnp.sum(lw, axis=0, keepdims=True) - cin)
    rt = r_ref[...] * p_in
    kt = kk_ref[...] * jnp.exp(cin - lw)
    kbar = kd_ref[...] * inv_in
    bbar = bb_ref[...] * inv_in
    khat = kd_ref[...] * p_rest
    bhat = bb_ref[...] * p_rest
    v = v_ref[...]

    n2 = 2 * c
    r2 = lax.broadcasted_iota(jnp.int32, (n2, n2), 0)
    c2 = lax.broadcasted_iota(jnp.int32, (n2, n2), 1)
    tdiff = sgn * ((r2 & (c - 1)) - (c2 & (c - 1)))
    strict = tdiff > 0
    incl = tdiff >= 0
    eye = (r2 == c2).astype(F32)
    first = lax.broadcasted_iota(jnp.int32, (1, LANES), 1) < half

    def stack(xp):
        return jnp.concatenate([jnp.where(first, xp, 0.0), jnp.where(first, 0.0, xp)], axis=0)

    pairs = range(hn // LANES)
    sls = [slice(p * LANES, (p + 1) * LANES) for p in pairs]
    v_s = [stack(v[:, sl]).astype(BF16) for sl in sls]
    lhs = [jnp.concatenate([stack(kt[:, sl]), stack(rt[:, sl])], axis=0).astype(BF16) for sl in sls]
    gb = [_dot_nt(lhs[p], stack(bbar[:, sls[p]]).astype(BF16)) for p in pairs]
    gk = [_dot_nt(lhs[p], stack(kbar[:, sls[p]]).astype(BF16)) for p in pairs]
    states = [st_scr[p] for p in pairs]
    m1 = [_dot(lhs[p], states[p].astype(BF16)) for p in pairs]
    a_m = [jnp.where(strict, gb[p][:n2], 0.0) for p in pairs]
    qb = [jnp.where(incl, gb[p][n2:], 0.0) for p in pairs]
    b_m = [jnp.where(strict, gk[p][:n2], 0.0).astype(BF16) for p in pairs]
    qk = [jnp.where(incl, gk[p][n2:], 0.0) for p in pairs]
    rhs = [m1[p][:n2] + _dot(b_m[p], v_s[p]) for p in pairs]
    t_inv = _unit_triangular_inverse(a_m, eye, r2, c2, c)
    u = [_bdot(t_inv[p], rhs[p]) for p in pairs]
    vu = [jnp.concatenate([v_s[p], u[p].astype(BF16)], axis=0) for p in pairs]
    for p in pairs:
        y = m1[p][n2:] + _dot(jnp.concatenate([qk[p], -qb[p]], axis=1).astype(BF16), vu[p])
        y_ref[:, sls[p]] = y[:c] + y[c:]
    for p in pairs:
        x_hat = jnp.concatenate([stack(khat[:, sls[p]]), -stack(bhat[:, sls[p]])], axis=0).astype(BF16)
        st_scr[p] = jnp.exp(ctot[sls[p]]) * states[p] + _dot_tn(x_hat, vu[p])

    @pl.when(ci == nc - 1)
    def _():
        sf_ref[...] = st_scr[...]


def _rwkv_scan(lw, kd, bb, v, kk, r, s0):
    _, b, l, hn = lw.shape
    c = RWKV_CHUNK
    nc = l // c
    npair = hn // LANES
    assert l % c == 0 and hn % LANES == 0

    def chunk(d, ci):
        return ci + d * (nc - 1 - 2 * ci)

    per_dir = pl.BlockSpec((None, None, c, hn), lambda d, bi, ci: (d, bi, chunk(d, ci), 0))
    shared = pl.BlockSpec((None, c, hn), lambda d, bi, ci: (bi, chunk(d, ci), 0))
    st = pl.BlockSpec((None, None, npair, LANES, LANES), lambda d, bi, ci: (d, bi, 0, 0, 0))
    return pl.pallas_call(
        _rwkv_scan_kernel,
        out_shape=(jax.ShapeDtypeStruct((2, b, l, hn), F32), jax.ShapeDtypeStruct(s0.shape, F32)),
        grid=(2, b, nc),
        in_specs=[per_dir, per_dir, per_dir, shared, shared, shared, st],
        out_specs=(per_dir, st),
        scratch_shapes=[pltpu.VMEM((npair, LANES, LANES), F32)],
        compiler_params=_params(3),
        name="rwkv_scan",
    )(lw, kd, bb, v, kk, r, s0)


def _rwkv_out_kernel(x_ref, y_ref, bonus_ref, z_ref, gate_ref, lng_ref, lnb_ref, e_ref, et_ref, w_out_ref, o_ref):
    y = y_ref[0] + y_ref[1]
    mean = _head_sum(y, e_ref, et_ref) * (1.0 / RWKV_HEAD)
    yc = y - mean
    var = _head_sum(yc * yc, e_ref, et_ref) * (1.0 / RWKV_HEAD)
    yn = yc * lax.rsqrt(var + RWKV_LN_EPS) * lng_ref[...] + lnb_ref[...]
    out = (yn + bonus_ref[...]) * _silu(z_ref[...])
    o_ref[...] = x_ref[...] + gate_ref[...] * _dot(out.astype(BF16), w_out_ref[...])


def _rwkv_output(x, y, bonus, z, gate, ln_gain, ln_bias, w_out):
    b, l, d = x.shape
    e = y.shape[-1]
    t = _tile(l, 512)
    ind, ind_t = _head_indicator(e, RWKV_HEAD)
    tok_d = pl.BlockSpec((None, t, d), lambda bi, ti: (bi, ti, 0))
    tok_e = pl.BlockSpec((None, t, e), lambda bi, ti: (bi, ti, 0))
    return pl.pallas_call(
        _rwkv_out_kernel,
        out_shape=jax.ShapeDtypeStruct(x.shape, F32),
        grid=(b, l // t),
        in_specs=[tok_d, pl.BlockSpec((2, None, t, e), lambda bi, ti: (0, bi, ti, 0)), tok_e, tok_e, _mod_spec(d),
                  _const_spec((1, e)), _const_spec((1, e)), _const_spec(ind.shape), _const_spec(ind_t.shape),
                  _const_spec(w_out.shape)],
        out_specs=tok_d,
        compiler_params=_params(2),
        name="rwkv_output",
    )(x, y, bonus, z, gate, ln_gain.reshape(1, e), ln_bias.reshape(1, e), ind, ind_t, w_out.astype(BF16))


def _mla_proj_kernel(*refs, need_q, rotate):
    if need_q:
        (x_ref, shift_ref, scale_ref, w_in_ref, qn_ref, kvn_ref, w_uq_ref, w_uk_ref, w_uvt_ref, gq_ref, gk_ref,
         ta_ref, tb_ref, q_o, k_o, vt_o, z_o) = refs
    else:
        (x_ref, shift_ref, scale_ref, w_in_ref, kvn_ref, w_uk_ref, w_uvt_ref, gk_ref, ta_ref, tb_ref, k_o, vt_o) = refs
    hb = (_rms(x_ref[...]) * (1.0 + scale_ref[...]) + shift_ref[...]).astype(BF16)
    o_kv = MLA_Q_RANK if need_q else 0
    o_kr = o_kv + MLA_KV_RANK
    first = lax.broadcasted_iota(jnp.int32, (1, LANES), 1) < MLA_ROPE
    ta = ta_ref[...]
    tb = tb_ref[...]

    def rope_tile(t2, gain_ref, use_rot, use_plain):
        sw = pltpu.roll(t2, MLA_ROPE, 1)
        g_a = gain_ref[1:2, :]
        g_b = gain_ref[2:3, :]
        m_b = tb * g_b
        if not use_rot:
            return sw * jnp.where(first, 0.0, m_b)
        if not use_plain:
            m_b = jnp.where(first, m_b, 0.0)
        return t2 * (ta * g_a) + sw * m_b

    ckv = (_rms(_dot(hb, w_in_ref[:, o_kv:o_kr])) * kvn_ref[...]).astype(BF16)
    vt = _dot_nt(w_uvt_ref[...], ckv)
    kr = _dot(hb, w_in_ref[:, o_kr:o_kr + LANES])
    kr_ssq = jnp.sum(jnp.where(first, kr * kr, 0.0), axis=-1, keepdims=True)
    kr_out = rope_tile(kr, gk_ref, rotate, not rotate)
    ones_rows = (lax.broadcasted_iota(jnp.int32, (MLA_VT_ROWS - MLA_VDIM, vt.shape[1]), 0) == 0).astype(vt_o.dtype)
    for h in range(MLA_HEADS):
        vt_o[h, :MLA_VDIM, :] = vt[h * MLA_VDIM:(h + 1) * MLA_VDIM].astype(vt_o.dtype)
        vt_o[h, MLA_VDIM:, :] = ones_rows
    for h in range(MLA_HEADS):
        kn = _dot(ckv, w_uk_ref[:, h * LANES:(h + 1) * LANES])
        rs = lax.rsqrt((jnp.sum(kn * kn, axis=-1, keepdims=True) + kr_ssq) / MLA_QK_DIM + RMS_EPS)
        k_o[h] = jnp.concatenate([kn * rs * gk_ref[0:1, :], kr_out * rs], axis=1).astype(k_o.dtype)
    if need_q:
        z_o[...] = _dot(hb, w_in_ref[:, o_kr + LANES:]).astype(z_o.dtype)
        cq = (_rms(_dot(hb, w_in_ref[:, :MLA_Q_RANK])) * qn_ref[...]).astype(BF16)
        for h in range(MLA_HEADS):
            t12 = _dot(cq, w_uq_ref[:, h * 2 * LANES:(h + 1) * 2 * LANES])
            t1 = t12[:, :LANES]
            t2 = t12[:, LANES:]
            ssq = jnp.sum(t1 * t1, axis=-1, keepdims=True) + jnp.sum(jnp.where(first, t2 * t2, 0.0), axis=-1, keepdims=True)
            rs = lax.rsqrt(ssq / MLA_QK_DIM + RMS_EPS) * (MLA_SCALE * math.log2(math.e))
            q_o[h] = jnp.concatenate([t1 * rs * gq_ref[0:1, :], rope_tile(t2, gq_ref, True, True) * rs],
                                     axis=1).astype(q_o.dtype)


def _swap_halves(w):
    half = w.shape[-1] // 2
    return jnp.concatenate([w[..., half:], w[..., :half]], axis=-1)


def _qk_gain_rows(g):
    gn, gr = g[:MLA_NOPE], g[MLA_NOPE:]
    zero = jnp.zeros_like(gr)
    rows = jnp.stack([gn, jnp.concatenate([gr, zero]), jnp.concatenate([_swap_halves(gr), gr])])
    return jnp.concatenate([rows, jnp.zeros((SUBLANES - 3, LANES), F32)])


def _rope_tables(length):
    rows = length // GRID_W
    row = jnp.repeat(jnp.arange(rows, dtype=F32), GRID_W)
    col = jnp.tile(jnp.arange(GRID_W, dtype=F32), rows)
    nf = MLA_ROPE // 4
    inv_freq = ROPE_BASE ** (-jnp.arange(nf, dtype=F32) / nf)
    ang = jnp.concatenate([row[:, None] * inv_freq, col[:, None] * inv_freq], axis=-1)
    cos, sin = jnp.cos(ang), jnp.sin(ang)
    ta = jnp.concatenate([cos, cos, jnp.zeros((length, MLA_ROPE), F32)], axis=-1)
    tb = jnp.concatenate([-sin, sin, jnp.ones((length, MLA_ROPE), F32)], axis=-1)
    return ta, tb


def _mla_project(x, shift, scale, w_in, q_norm, kv_norm, w_uq, w_ukv, gq, gk, ta, tb, need_q, rotate):
    b, l, d = x.shape
    hh = MLA_HEADS
    t = _tile(l, MLA_KV_TILE)
    o1, o2, o3 = MLA_Q_RANK, MLA_Q_RANK + MLA_KV_RANK, MLA_Q_RANK + MLA_KV_RANK + MLA_ROPE
    w_kr = w_in[:, o2:o3]
    cols = ([w_in[:, :o1]] if need_q else []) + [w_in[:, o1:o2], w_kr, _swap_halves(w_kr)] + ([w_in[:, o3:]] if need_q else [])
    w_in_p = jnp.concatenate(cols, axis=1).astype(BF16)
    tok = pl.BlockSpec((None, t, d), lambda bi, ti: (bi, ti, 0))
    tab = pl.BlockSpec((t, LANES), lambda bi, ti: (ti, 0))
    head_qk = pl.BlockSpec((None, hh, t, 2 * LANES), lambda bi, ti: (bi, 0, ti, 0))
    head_vt = pl.BlockSpec((None, hh, None, MLA_VT_ROWS, t), lambda bi, ti: (bi, 0, ti, 0, 0))
    qk_shape = jax.ShapeDtypeStruct((b, hh, l, 2 * LANES), BF16)
    vt_shape = jax.ShapeDtypeStruct((b, hh, l // t, MLA_VT_ROWS, t), BF16)
    kvn = kv_norm.reshape(1, -1)
    wkv = w_ukv.reshape(MLA_KV_RANK, hh, MLA_NOPE + MLA_VDIM)
    w_uk = wkv[..., :MLA_NOPE].reshape(MLA_KV_RANK, hh * MLA_NOPE).astype(BF16)
    w_uvt = wkv[..., MLA_NOPE:].reshape(MLA_KV_RANK, hh * MLA_VDIM).T.astype(BF16)
    gk_rows = _qk_gain_rows(gk)
    if need_q:
        wq = w_uq.reshape(MLA_Q_RANK, hh, MLA_QK_DIM)
        w_uq_p = jnp.concatenate([wq, _swap_halves(wq[..., MLA_NOPE:])], axis=-1).reshape(MLA_Q_RANK, hh * 2 * LANES)
        width = w_in.shape[1] - o3
        ins = (x, shift, scale, w_in_p, q_norm.reshape(1, -1), kvn, w_uq_p.astype(BF16), w_uk, w_uvt,
               _qk_gain_rows(gq), gk_rows, ta, tb)
        in_specs = [tok, _mod_spec(d), _mod_spec(d), _const_spec(w_in_p.shape), _const_spec((1, MLA_Q_RANK)),
                    _const_spec(kvn.shape), _const_spec(w_uq_p.shape), _const_spec(w_uk.shape), _const_spec(w_uvt.shape),
                    _const_spec((SUBLANES, LANES)), _const_spec((SUBLANES, LANES)), tab, tab]
        out_shape = (qk_shape, qk_shape, vt_shape, jax.ShapeDtypeStruct((b, l, width), BF16))
        out_specs = (head_qk, head_qk, head_vt, pl.BlockSpec((None, t, width), lambda bi, ti: (bi, ti, 0)))
    else:
        ins = (x, shift, scale, w_in_p, kvn, w_uk, w_uvt, gk_rows, ta, tb)
        in_specs = [tok, _mod_spec(d), _mod_spec(d), _const_spec(w_in_p.shape), _const_spec(kvn.shape),
                    _const_spec(w_uk.shape), _const_spec(w_uvt.shape), _const_spec((SUBLANES, LANES)), tab, tab]
        out_shape = (qk_shape, vt_shape)
        out_specs = (head_qk, head_vt)
    return pl.pallas_call(
        functools.partial(_mla_proj_kernel, need_q=need_q, rotate=rotate),
        out_shape=out_shape,
        grid=(b, l // t),
        in_specs=in_specs,
        out_specs=out_specs,
        compiler_params=_params(2),
        name="mla_project_q" if need_q else "mla_project_kv",
    )(*ins)


def _flash_kernel(q_ref, kc_ref, vc_ref, kl_ref, vl_ref, o_ref,
                  m_scr, acc_scr, sa_scr, sb_scr, ma_scr, mb_scr, pa_scr, pb_scr):
    n_tiles, _, tt = vl_ref.shape
    tk = sa_scr.shape[0]
    per_blk = tk // tt
    nkv = n_tiles // per_blk
    dv = o_ref.shape[-1]
    q = q_ref[...]

    def put_scores(j, s_scr, bm_scr):
        s = _dot_nt(kl_ref[pl.ds(pl.multiple_of(j * tk, tk), tk), :], q)
        s_scr[...] = s
        bm_scr[...] = jnp.max(s, axis=0, keepdims=True)

    def accumulate(s_scr, bm_scr, p_scr, j):
        m_prev = m_scr[...]
        m_new = jnp.maximum(m_prev, bm_scr[...])
        p_scr[...] = jnp.exp2(s_scr[...] - m_new).astype(BF16)
        pv = _dot(vl_ref[j * per_blk], p_scr[0:tt, :])
        for i in range(1, per_blk):
            pv = pv + _dot(vl_ref[j * per_blk + i], p_scr[i * tt:(i + 1) * tt, :])
        acc_scr[...] = jnp.exp2(m_prev - m_new) * acc_scr[...] + pv
        m_scr[...] = m_new

    put_scores(0, sa_scr, ma_scr)
    st = _dot_nt(kc_ref[...], q)
    m0 = jnp.max(st, axis=0, keepdims=True)
    m_scr[...] = m0
    acc_scr[...] = _dot(vc_ref[0], jnp.exp2(st - m0).astype(BF16))

    for j in range(nkv):
        if j + 1 < nkv:
            put_scores(j + 1, sb_scr if j % 2 == 0 else sa_scr, mb_scr if j % 2 == 0 else ma_scr)
        if j % 2 == 0:
            accumulate(sa_scr, ma_scr, pa_scr, j)
        else:
            accumulate(sb_scr, mb_scr, pb_scr, j)
    o_ref[...] = jnp.transpose(acc_scr[:dv, :] / acc_scr[dv:dv + 1, :]).astype(o_ref.dtype)


def _mla_attention(q, k_ctx, vt_ctx, k_lat, vt_lat):
    b, hh, l, dq = q.shape
    lc = k_ctx.shape[2]
    _, _, nkv, dva, tt = vt_lat.shape
    dv = MLA_VDIM
    tq = _tile(l, 2048)
    tk = tt * (2 if nkv % 2 == 0 else 1)
    return pl.pallas_call(
        _flash_kernel,
        out_shape=jax.ShapeDtypeStruct((b, l, hh * dv), BF16),
        grid=(b, hh, l // tq),
        in_specs=[pl.BlockSpec((None, None, tq, dq), lambda bi, h, qi: (bi, h, qi, 0)),
                  pl.BlockSpec((None, None, lc, dq), lambda bi, h, qi: (bi, h, 0, 0)),
                  pl.BlockSpec((None, None, 1, dva, lc), lambda bi, h, qi: (bi, h, 0, 0, 0)),
                  pl.BlockSpec((None, None, l, dq), lambda bi, h, qi: (bi, h, 0, 0)),
                  pl.BlockSpec((None, None, nkv, dva, tt), lambda bi, h, qi: (bi, h, 0, 0, 0))],
        out_specs=pl.BlockSpec((None, tq, dv), lambda bi, h, qi: (bi, qi, h)),
        scratch_shapes=[pltpu.VMEM((1, tq), F32), pltpu.VMEM((dva, tq), F32),
                        pltpu.VMEM((tk, tq), F32), pltpu.VMEM((tk, tq), F32),
                        pltpu.VMEM((1, tq), F32), pltpu.VMEM((1, tq), F32),
                        pltpu.VMEM((tk, tq), BF16), pltpu.VMEM((tk, tq), BF16)],
        compiler_params=_params(3),
        name="mla_attention",
    )(q, k_ctx, vt_ctx, k_lat, vt_lat)


def _gated_out_kernel(x_ref, a_ref, z_ref, gate_ref, w_ref, o_ref):
    s = (a_ref[...].astype(F32) * _silu(z_ref[...].astype(F32))).astype(BF16)
    o_ref[...] = x_ref[...] + gate_ref[...] * _dot(s, w_ref[...])


def _gated_out(x, a, z, gate, w_out):
    b, l, d = x.shape
    e = a.shape[-1]
    t = _tile(l, 512)
    tok_d = pl.BlockSpec((None, t, d), lambda bi, ti: (bi, ti, 0))
    tok_e = pl.BlockSpec((None, t, e), lambda bi, ti: (bi, ti, 0))
    return pl.pallas_call(
        _gated_out_kernel,
        out_shape=jax.ShapeDtypeStruct(x.shape, F32),
        grid=(b, l // t),
        in_specs=[tok_d, tok_e, tok_e, _mod_spec(d), _const_spec(w_out.shape)],
        out_specs=tok_d,
        compiler_params=_params(2),
        name="mla_output",
    )(x, a, z, gate, w_out.astype(BF16))


def kernel(x, c, ctx, c_ctx, ada_w, ada_b, sgu_w_in, sgu_gain, sgu_w_s, sgu_b_s, sgu_w_out, rwkv_mu, rwkv_w_in, rwkv_w_lora1, rwkv_w_lora2, rwkv_w0, rwkv_a_lora1, rwkv_a_lora2, rwkv_a0, rwkv_k_k, rwkv_k_a, rwkv_r_k, rwkv_ln_gain, rwkv_ln_bias, rwkv_w_out, mla_w_in, mla_q_norm, mla_kv_norm, mla_w_uq, mla_w_ukv, mla_qk_gain_q, mla_qk_gain_k, mla_w_out):
    bsz, length, d = x.shape
    depth = ada_w.shape[0]
    n_cond = -(-(bsz + 1) // SUBLANES) * SUBLANES
    cond = jnp.concatenate([c, c_ctx[None], jnp.zeros((n_cond - bsz - 1, d), F32)])
    mods = _ada_modulation(cond, ada_w, ada_b)
    ctx_readers = [i for i in range(depth) if i % N_MIXERS != 0]
    last_ctx_reader = ctx_readers[-1] if ctx_readers else -1
    ta, tb = _rope_tables(length)
    for i in range(depth):
        kind, j = i % N_MIXERS, i // N_MIXERS
        update_ctx = i < last_ctx_reader
        shift, scale, gate = (mods[i, :bsz, s * d:(s + 1) * d][:, None, :] for s in range(3))
        c_shift, c_scale, c_gate = (jnp.broadcast_to(mods[i, bsz, s * d:(s + 1) * d], (bsz, 1, d)) for s in range(3))
        if kind == 0:
            sgu_args = (sgu_w_in[j], sgu_gain[j], sgu_w_s[j], sgu_b_s[j], sgu_w_out[j])
            x = _sgu_layer(x, shift, scale, gate, *sgu_args)
            if update_ctx:
                ctx = _sgu_layer(ctx, c_shift, c_scale, c_gate, *sgu_args)
        elif kind == 1:
            feat_args = (rwkv_mu[j], rwkv_w_in[j], rwkv_w_lora1[j], rwkv_w_lora2[j], rwkv_w0[j],
                         rwkv_a_lora1[j], rwkv_a_lora2[j], rwkv_a0[j], rwkv_k_k[j], rwkv_k_a[j],
                         rwkv_r_k[j].reshape(-1))
            out_args = (rwkv_ln_gain[j], rwkv_ln_bias[j], rwkv_w_out[j])
            r_c, v_c, kk_c, z_c, bonus_c, lw_c, kd_c, bb_c = _rwkv_features(ctx, c_shift, c_scale, *feat_args)
            state0 = jnp.zeros((2, bsz, r_c.shape[-1] // LANES, LANES, LANES), F32)
            y_c, s_ctx = _rwkv_scan(lw_c, kd_c, bb_c, v_c, kk_c, r_c, state0)
            r_l, v_l, kk_l, z_l, bonus_l, lw_l, kd_l, bb_l = _rwkv_features(x, shift, scale, *feat_args)
            y_l, _ = _rwkv_scan(lw_l, kd_l, bb_l, v_l, kk_l, r_l, s_ctx)
            x = _rwkv_output(x, y_l, bonus_l, z_l, gate, *out_args)
            if update_ctx:
                ctx = _rwkv_output(ctx, y_c, bonus_c, z_c, c_gate, *out_args)
        else:
            assert not update_ctx, "context-stream update after an MLA layer is not implemented"
            mla_args = (mla_w_in[j], mla_q_norm[j], mla_kv_norm[j], mla_w_uq[j], mla_w_ukv[j],
                        mla_qk_gain_q[j], mla_qk_gain_k[j])
            n_ctx = ctx.shape[1]
            k_c, v_c = _mla_project(ctx, c_shift, c_scale, *mla_args, ta[:n_ctx], tb[:n_ctx], need_q=False, rotate=False)
            q_l, k_l, v_l, z_l = _mla_project(x, shift, scale, *mla_args, ta, tb, need_q=True, rotate=True)
            o_l = _mla_attention(q_l, k_c, v_c, k_l, v_l)
            x = _gated_out(x, o_l, z_l, gate, mla_w_out[j])
    return x
```
